```python
import math
import jax
import jax.numpy as jnp
from jax import lax
import numpy as np

D_MODEL = 1024
BATCH = 8
SEQ = 4096
DEPTH = 4

GRID_W = 64
CTX_LEN = 256
EPS = 1e-6
CONV_DIM = 256
CONV_K = 3
NA_HEADS = 8
NA_HEAD_DIM = 64
NA_DIM = NA_HEADS * NA_HEAD_DIM
WIN_H = 8
WIN_W = 16
SSD_HEADS = 4
SSD_HEAD_DIM = 64
SSD_DIM = SSD_HEADS * SSD_HEAD_DIM
SSD_GROUPS = 2
SSD_STATE = 128
SSD_CONV_K = 3
SSD_CHUNK = 128
XBC_DIM = SSD_DIM + 2 * SSD_GROUPS * SSD_STATE
D_MIX = CONV_DIM + NA_DIM + SSD_DIM
ROPE_BASE = 10000.0
DT_MIN = 1e-3
DT_MAX = 1e-1
IN_SPLITS = (CONV_DIM, CONV_DIM, CONV_DIM, CONV_DIM, NA_DIM, NA_DIM, NA_DIM, NA_DIM, XBC_DIM, SSD_DIM, 2 * SSD_HEADS)
IN_DIM = 4 * CONV_DIM + 4 * NA_DIM + XBC_DIM + SSD_DIM + 2 * SSD_HEADS
F32 = jnp.float32

kernel_name = 'hybrid_conv_natten_ssd_dit'


def rms_norm(x, g):
    xf = x.astype(F32)
    y = xf * lax.rsqrt(jnp.mean(jnp.square(xf), axis=-1, keepdims=True) + EPS)
    return (y * g.astype(F32)).astype(x.dtype)


def depthwise_conv(x, w, b=None):
    k = w.shape[0]
    pad = k // 2
    y = lax.conv_general_dilated(x, w.astype(x.dtype)[:, None, :], window_strides=(1,),
                                 padding=[(pad, k - 1 - pad)],
                                 dimension_numbers=('NWC', 'WIO', 'NWC'),
                                 feature_group_count=x.shape[-1])
    if b is not None:
        y = y + b.astype(x.dtype)
    return y


def split_heads(t, n_heads):
    return t.reshape(*t.shape[:-1], n_heads, t.shape[-1] // n_heads)


def split_projection(u):
    return jnp.split(u, np.cumsum(IN_SPLITS)[:-1].tolist(), axis=-1)


def axial_rope(t, row_pos, col_pos):
    n_freq = t.shape[-1] // 4
    inv_freq = ROPE_BASE ** (-jnp.arange(n_freq, dtype=F32) / n_freq)
    ang = jnp.stack([row_pos[:, None] * inv_freq, col_pos[:, None] * inv_freq], axis=1)
    cos = jnp.cos(ang)[None, :, None]
    sin = jnp.sin(ang)[None, :, None]
    tr = t.astype(F32).reshape(*t.shape[:-1], 2, 2, n_freq)
    t1, t2 = tr[..., 0, :], tr[..., 1, :]
    out = jnp.stack([t1 * cos - t2 * sin, t1 * sin + t2 * cos], axis=-2)
    return out.reshape(t.shape).astype(t.dtype)


def ssd_chunked(xh, dt, a, bh, ch, init_state):
    bsz, seq = xh.shape[:2]
    nc = seq // SSD_CHUNK
    chunk = lambda t: t.astype(F32).reshape(bsz, nc, SSD_CHUNK, *t.shape[2:])
    x_c, dt_c, b_c, c_c = chunk(xh), chunk(dt), chunk(bh), chunk(ch)
    acs = jnp.cumsum(dt_c * a, axis=2)
    lower = jnp.tril(jnp.ones((SSD_CHUNK, SSD_CHUNK), dtype=bool))
    seg = acs[:, :, :, None, :] - acs[:, :, None, :, :]
    decay = jnp.exp(jnp.where(lower[None, None, :, :, None], seg, -jnp.inf))
    mix = jnp.einsum('bcihn,bcjhn->bcijh', c_c, b_c) * decay * dt_c[:, :, None, :, :]
    y_diag = jnp.einsum('bcijh,bcjhp->bcihp', mix, x_c)
    w_end = jnp.exp(acs[:, :, -1:, :] - acs) * dt_c
    states = jnp.einsum('bcjhn,bcjhp->bchpn', b_c * w_end[..., None], x_c)
    chunk_decay = jnp.exp(acs[:, :, -1, :])

    def step(s, inp):
        st, dec = inp
        return s * dec[:, :, None, None] + st, s

    final, prev = lax.scan(step, init_state.astype(F32),
                           (jnp.moveaxis(states, 1, 0), jnp.moveaxis(chunk_decay, 1, 0)))
    prev = jnp.moveaxis(prev, 0, 1)
    y_off = jnp.einsum('bcihn,bchpn->bcihp', c_c * jnp.exp(acs)[..., None], prev)
    return (y_diag + y_off).reshape(bsz, seq, *xh.shape[2:]), final


def ssd_bidirectional(xh, bh, ch, dt_f, dt_b, a_f, a_b, init_f, init_b):
    flip = lambda t: jnp.flip(t, axis=1)
    y_f, fin_f = ssd_chunked(xh, dt_f, a_f, bh, ch, init_f)
    y_b, fin_b = ssd_chunked(flip(xh), flip(dt_b), a_b, flip(bh), flip(ch), init_b)
    return y_f + flip(y_b), fin_f, fin_b


def ssd_inputs(u_xbc, u_dt, conv_w, conv_b, dt_bias_f, dt_bias_b, row_pos=None, col_pos=None):
    xbc = jax.nn.silu(depthwise_conv(u_xbc, conv_w, conv_b))
    xs, bs, cs = jnp.split(xbc, [SSD_DIM, SSD_DIM + SSD_GROUPS * SSD_STATE], axis=-1)
    bs = split_heads(bs, SSD_GROUPS)
    cs = split_heads(cs, SSD_GROUPS)
    if row_pos is not None:
        bs = axial_rope(bs, row_pos, col_pos)
        cs = axial_rope(cs, row_pos, col_pos)
    rep = SSD_HEADS // SSD_GROUPS
    bh = jnp.repeat(bs, rep, axis=2)
    ch = jnp.repeat(cs, rep, axis=2)
    xh = split_heads(xs, SSD_HEADS)
    dt_f = jax.nn.softplus(u_dt[..., :SSD_HEADS].astype(F32) + dt_bias_f.astype(F32))
    dt_b = jax.nn.softplus(u_dt[..., SSD_HEADS:].astype(F32) + dt_bias_b.astype(F32))
    return xh, bh, ch, dt_f, dt_b


def ssd_finish(y, xh, z, ssd_d, norm_w):
    y = y + ssd_d.astype(F32)[:, None] * xh.astype(F32)
    y = y.reshape(*y.shape[:2], SSD_DIM).astype(z.dtype)
    return rms_norm(y * jax.nn.silu(z), norm_w)


def neighborhood_attention(q, k, v, k_ctx, v_ctx, rpb):
    bsz, seq, heads, dh = q.shape
    rows = seq // GRID_W
    kh = min(WIN_H, rows)
    kw = WIN_W
    scale = dh ** -0.5
    qg = q.reshape(bsz, rows, GRID_W, heads, dh)
    kg = k.reshape(bsz, rows, GRID_W, heads, dh)
    vg = v.reshape(bsz, rows, GRID_W, heads, dh)
    cols = jnp.arange(GRID_W)
    col_idx = jnp.clip(cols - kw // 2, 0, GRID_W - kw)[:, None] + jnp.arange(kw)[None, :]
    rpb_cols = rpb[:, :, col_idx - cols[:, None] + (WIN_W - 1)]
    n_win = kh * kw

    def row_block(r):
        r0 = jnp.clip(r - kh // 2, 0, rows - kh)
        q_r = lax.dynamic_index_in_dim(qg, r, axis=1, keepdims=False)
        k_win = lax.dynamic_slice_in_dim(kg, r0, kh, axis=1)[:, :, col_idx]
        v_win = lax.dynamic_slice_in_dim(vg, r0, kh, axis=1)[:, :, col_idx]
        bias = jnp.take(rpb_cols, r0 + jnp.arange(kh) - r + (WIN_H - 1), axis=1)
        s_win = (jnp.einsum('bqhd,biqjhd->bhqij', q_r, k_win).astype(F32) * scale
                 + bias.transpose(0, 2, 1, 3)[None].astype(F32))
        s_ctx = jnp.einsum('bqhd,bkhd->bhqk', q_r, k_ctx).astype(F32) * scale
        s = jnp.concatenate([s_win.reshape(bsz, heads, GRID_W, n_win), s_ctx], axis=-1)
        p = jax.nn.softmax(s, axis=-1).astype(v.dtype)
        p_win = p[..., :n_win].reshape(bsz, heads, GRID_W, kh, kw)
        return (jnp.einsum('bhqij,biqjhd->bqhd', p_win, v_win)
                + jnp.einsum('bhqk,bkhd->bqhd', p[..., n_win:], v_ctx))

    out = lax.map(row_block, jnp.arange(rows))
    return jnp.moveaxis(out, 0, 1).reshape(bsz, seq, heads * dh)


def context_attention(q, k, v):
    s = jnp.einsum('bqhd,bkhd->bhqk', q, k).astype(F32) * (q.shape[-1] ** -0.5)
    p = jax.nn.softmax(s, axis=-1).astype(v.dtype)
    o = jnp.einsum('bhqk,bkhd->bqhd', p, v)
    return o.reshape(*o.shape[:2], NA_DIM)


def merge_project(y_sc, z_sc, y_na, z_na, y_ss, w_out):
    y = jnp.concatenate([y_sc * jax.nn.silu(z_sc), y_na * jax.nn.silu(z_na), y_ss], axis=-1)
    return y @ w_out


def hybrid_layer(x, x_ctx, c, c_ctx, w_ada, b_ada, g_pre, g_post, w_in, conv_a_w, rpb,
                 ssd_conv_w, ssd_conv_b, dt_bias_f, dt_bias_b, a_log_f, a_log_b, ssd_d,
                 ssd_norm_w, w_out, row_pos, col_pos, update_ctx):
    bsz = x.shape[0]
    shift, scale, gate = jnp.split((jax.nn.silu(c) @ w_ada + b_ada)[:, None, :], 3, axis=-1)
    shift_c, scale_c, gate_c = jnp.split(jax.nn.silu(c_ctx) @ w_ada + b_ada, 3, axis=-1)
    h = rms_norm(x, g_pre) * (1 + scale) + shift
    h_c = rms_norm(x_ctx, g_pre) * (1 + scale_c) + shift_c
    (sc_in, sc_b, sc_c, sc_z, na_q, na_k, na_v, na_z, ss_xbc, ss_z, ss_dt) = split_projection(h @ w_in)
    (c_sc_in, c_sc_b, c_sc_c, c_sc_z, c_na_q, c_na_k, c_na_v, c_na_z,
     c_ss_xbc, c_ss_z, c_ss_dt) = split_projection(h_c @ w_in)
    decay_f = -jnp.exp(a_log_f.astype(F32))
    decay_b = -jnp.exp(a_log_b.astype(F32))

    c_xh, c_bh, c_ch, c_dt_f, c_dt_b = ssd_inputs(c_ss_xbc, c_ss_dt, ssd_conv_w, ssd_conv_b, dt_bias_f, dt_bias_b)
    zero_state = jnp.zeros((bsz, SSD_HEADS, SSD_HEAD_DIM, SSD_STATE), F32)
    c_y_ss, fin_f, fin_b = ssd_bidirectional(c_xh, c_bh, c_ch, c_dt_f, c_dt_b, decay_f, decay_b,
                                             zero_state, zero_state)
    c_k = split_heads(c_na_k, NA_HEADS)
    c_v = split_heads(c_na_v, NA_HEADS)

    y_sc = sc_b * depthwise_conv(sc_c * sc_in, conv_a_w)
    y_na = neighborhood_attention(split_heads(na_q, NA_HEADS), split_heads(na_k, NA_HEADS),
                                  split_heads(na_v, NA_HEADS), c_k, c_v, rpb)
    xh, bh, ch, dt_f, dt_b = ssd_inputs(ss_xbc, ss_dt, ssd_conv_w, ssd_conv_b, dt_bias_f, dt_bias_b,
                                        row_pos, col_pos)
    y_ss, _, _ = ssd_bidirectional(xh, bh, ch, dt_f, dt_b, decay_f, decay_b, fin_f, fin_b)
    out = merge_project(y_sc, sc_z, y_na, na_z, ssd_finish(y_ss, xh, ss_z, ssd_d, ssd_norm_w), w_out)
    x = x + gate * rms_norm(out, g_post)

    if update_ctx:
        c_y_sc = c_sc_b * depthwise_conv(c_sc_c * c_sc_in, conv_a_w)
        c_y_na = context_attention(split_heads(c_na_q, NA_HEADS), c_k, c_v)
        c_out = merge_project(c_y_sc, c_sc_z, c_y_na, c_na_z,
                              ssd_finish(c_y_ss, c_xh, c_ss_z, ssd_d, ssd_norm_w), w_out)
        x_ctx = x_ctx + gate_c * rms_norm(c_out, g_post)
    return x, x_ctx


def setup_inputs(seed: int = 0) -> dict:
    key = jax.random.key(seed)
    ks = jax.random.split(key, 24)
    nrm = lambda k, shape, s: jax.random.normal(k, shape, F32) * s

    def dt_bias(k):
        dt = jnp.exp(jax.random.uniform(k, (DEPTH, SSD_HEADS), F32, math.log(DT_MIN), math.log(DT_MAX)))
        return dt + jnp.log(-jnp.expm1(-dt))

    return {
        'x': nrm(ks[0], (BATCH, SEQ, D_MODEL), 1.0),
        'c': nrm(ks[1], (BATCH, D_MODEL), 1.0),
        'ctx': nrm(ks[2], (BATCH, CTX_LEN, D_MODEL), 1.0),
        'c_ctx': nrm(ks[3], (D_MODEL,), 1.0),
        'w_ada': nrm(ks[4], (DEPTH, D_MODEL, 3 * D_MODEL), 0.5 * D_MODEL ** -0.5),
        'b_ada': nrm(ks[5], (DEPTH, 3 * D_MODEL), 0.01),
        'g_pre': 1.0 + nrm(ks[6], (DEPTH, D_MODEL), 0.05),
        'g_post': 1.0 + nrm(ks[7], (DEPTH, D_MODEL), 0.05),
        'w_in': nrm(ks[8], (DEPTH, D_MODEL, IN_DIM), D_MODEL ** -0.5),
        'conv_a_w': nrm(ks[9], (DEPTH, CONV_K, CONV_DIM), CONV_K ** -0.5),
        'rpb': nrm(ks[10], (DEPTH, NA_HEADS, 2 * WIN_H - 1, 2 * WIN_W - 1), 0.1),
        'ssd_conv_w': nrm(ks[11], (DEPTH, SSD_CONV_K, XBC_DIM), SSD_CONV_K ** -0.5),
        'ssd_conv_b': nrm(ks[12], (DEPTH, XBC_DIM), 0.01),
        'dt_bias_f': dt_bias(ks[13]),
        'dt_bias_b': dt_bias(ks[14]),
        'a_log_f': jnp.log(jax.random.uniform(ks[15], (DEPTH, SSD_HEADS), F32, 1.0, 16.0)),
        'a_log_b': jnp.log(jax.random.uniform(ks[16], (DEPTH, SSD_HEADS), F32, 1.0, 16.0)),
        'ssd_d': 1.0 + nrm(ks[17], (DEPTH, SSD_HEADS), 0.05),
        'ssd_norm_w': 1.0 + nrm(ks[18], (DEPTH, SSD_DIM), 0.05),
        'w_out': nrm(ks[19], (DEPTH, D_MIX, D_MODEL), D_MIX ** -0.5),
    }


def reference(x, c, ctx, c_ctx, w_ada, b_ada, g_pre, g_post, w_in, conv_a_w, rpb, ssd_conv_w,
              ssd_conv_b, dt_bias_f, dt_bias_b, a_log_f, a_log_b, ssd_d, ssd_norm_w, w_out):
    pos = jnp.arange(x.shape[1])
    row_pos = (pos // GRID_W).astype(F32)
    col_pos = (pos % GRID_W).astype(F32)
    x_ctx = ctx
    for layer in range(DEPTH):
        x, x_ctx = hybrid_layer(x, x_ctx, c, c_ctx, w_ada[layer], b_ada[layer], g_pre[layer],
                                g_post[layer], w_in[layer], conv_a_w[layer], rpb[layer],
                                ssd_conv_w[layer], ssd_conv_b[layer], dt_bias_f[layer],
                                dt_bias_b[layer], a_log_f[layer], a_log_b[layer], ssd_d[layer],
                                ssd_norm_w[layer], w_out[layer], row_pos, col_pos,
                                layer < DEPTH - 1)
    return x
```

```python
import functools
import math

import jax
import jax.numpy as jnp
import numpy as np
from jax import lax
from jax.experimental import pallas as pl
from jax.experimental.pallas import tpu as pltpu

F32 = jnp.float32
BF16 = jnp.bfloat16
HIGHEST = lax.Precision.HIGHEST

D_MODEL = 1024
GRID_W = 64
EPS = 1e-6
CONV_DIM = 256
NA_HEADS = 8
NA_HEAD_DIM = 64
NA_DIM = NA_HEADS * NA_HEAD_DIM
WIN_H = 8
WIN_W = 16
SSD_HEADS = 4
SSD_HEAD_DIM = 64
SSD_DIM = SSD_HEADS * SSD_HEAD_DIM
SSD_GROUPS = 2
SSD_STATE = 128
SSD_CHUNK = 128
XBC_DIM = SSD_DIM + 2 * SSD_GROUPS * SSD_STATE
ROPE_BASE = 10000.0
U_DIM = 4 * CONV_DIM + 4 * NA_DIM + XBC_DIM + SSD_DIM
DT_PAD = 128
COL_Q = 4 * CONV_DIM
COL_K = COL_Q + NA_DIM
COL_V = COL_K + NA_DIM
COL_ZNA = COL_V + NA_DIM
COL_XBC = COL_ZNA + NA_DIM
COL_ZSS = COL_XBC + XBC_DIM

LANE = 128
ATTN_ROWS = 4
ATTN_KEY_ROWS = 12
VMEM_LIMIT = 56 * 1024 * 1024

NEG_INF = float("-inf")


def _sigmoid(x):
    return 1.0 / (1.0 + jnp.exp(-x))


def _silu(x):
    return x * _sigmoid(x)


def _softplus(x):
    return jnp.maximum(x, 0.0) + jnp.log1p(jnp.exp(-jnp.abs(x)))


def _rms(x):
    return x * lax.rsqrt(jnp.mean(x * x, axis=-1, keepdims=True) + EPS)


def _params(*sem):
    return pltpu.CompilerParams(dimension_semantics=sem, vmem_limit_bytes=VMEM_LIMIT)


def _ada_kernel(cc_ref, w_ref, b_ref, o_ref):
    s = _silu(cc_ref[...])
    o_ref[...] = jnp.dot(s, w_ref[...], preferred_element_type=F32, precision=HIGHEST) + b_ref[...]


def _ada_all_layers(cc, w_ada, b_ada):
    depth = w_ada.shape[0]
    r = cc.shape[0]
    nblk = 3 * D_MODEL // D_MODEL
    return pl.pallas_call(
        _ada_kernel,
        out_shape=jax.ShapeDtypeStruct((depth, r, 3 * D_MODEL), F32),
        grid=(depth, nblk),
        in_specs=[
            pl.BlockSpec((r, D_MODEL), lambda l, n: (0, 0)),
            pl.BlockSpec((None, D_MODEL, D_MODEL), lambda l, n: (l, 0, n)),
            pl.BlockSpec((None, 1, D_MODEL), lambda l, n: (l, 0, n)),
        ],
        out_specs=pl.BlockSpec((None, r, D_MODEL), lambda l, n: (l, 0, n)),
        compiler_params=_params("arbitrary", "arbitrary"),
        name="ada",
    )(cc, w_ada, b_ada.reshape(depth, 1, 3 * D_MODEL))


def _inproj_kernel(x_ref, shift_ref, scale_ref, g_ref, w_ref, wdt_ref, u_ref, dt_ref):
    h = _rms(x_ref[...]) * g_ref[...]
    h = h * (1.0 + scale_ref[...]) + shift_ref[...]
    hb = h.astype(BF16)
    nb = 512
    for n in range(0, U_DIM, nb):
        u_ref[:, n:n + nb] = jnp.dot(hb, w_ref[:, n:n + nb], preferred_element_type=F32).astype(BF16)
    dt_ref[...] = jnp.dot(hb, wdt_ref[...], preferred_element_type=F32)


def _inproj(x, shift, scale, g_pre, w_u, w_dt, tm):
    b, l, d = x.shape
    return pl.pallas_call(
        _inproj_kernel,
        out_shape=(jax.ShapeDtypeStruct((b, l, U_DIM), BF16), jax.ShapeDtypeStruct((b, l, DT_PAD), F32)),
        grid=(b, l // tm),
        in_specs=[
            pl.BlockSpec((None, tm, d), lambda i, j: (i, j, 0)),
            pl.BlockSpec((None, 1, d), lambda i, j: (i, 0, 0)),
            pl.BlockSpec((None, 1, d), lambda i, j: (i, 0, 0)),
            pl.BlockSpec((1, d), lambda i, j: (0, 0)),
            pl.BlockSpec((d, U_DIM), lambda i, j: (0, 0)),
            pl.BlockSpec((d, DT_PAD), lambda i, j: (0, 0)),
        ],
        out_specs=(
            pl.BlockSpec((None, tm, U_DIM), lambda i, j: (i, j, 0)),
            pl.BlockSpec((None, tm, DT_PAD), lambda i, j: (i, j, 0)),
        ),
        compiler_params=_params("arbitrary", "arbitrary"),
        name="inproj",
    )(x, shift, scale, g_pre, w_u, w_dt)


def _attn_variants(n_rows):
    return {
        "top": dict(jlo=[0, 0, 0, 0], droff=0),
        "mid": dict(jlo=[0, 1, 2, 3], droff=-WIN_H // 2),
        "bot": dict(jlo=[4, 4, 4, 4], droff=-WIN_H),
    }


def _head_scales():
    lane1 = lax.broadcasted_iota(jnp.int32, (1, LANE), 1)
    scale = NA_HEAD_DIM ** -0.5
    lo = jnp.where(lane1 < NA_HEAD_DIM, scale, 0.0).astype(BF16)
    hi = jnp.where(lane1 >= NA_HEAD_DIM, scale, 0.0).astype(BF16)
    return lo, hi


def _softmax_strip(s_scr, p_scr, l_scr, bias_ref, hh, row0, blocks, n_ctx_blocks, n_win_blocks):
    rows = slice(row0, row0 + 32)
    brow = (row0 % GRID_W)
    lane = lax.broadcasted_iota(jnp.int32, (32, LANE), 1)
    vals = []
    for m, d, keep in blocks:
        s = s_scr[rows, m * LANE:(m + 1) * LANE] + bias_ref[hh, d, brow:brow + 32, :]
        if keep == "lo":
            s = jnp.where(lane < GRID_W, s, NEG_INF)
        elif keep == "hi":
            s = jnp.where(lane >= GRID_W, s, NEG_INF)
        vals.append((m, s))
    for c in range(n_ctx_blocks):
        m = n_win_blocks + c
        vals.append((m, s_scr[rows, m * LANE:(m + 1) * LANE]))
    mx = vals[0][1]
    for _, s in vals[1:]:
        mx = jnp.maximum(mx, s)
    mx = jnp.max(mx, axis=-1, keepdims=True)
    tot = None
    used = set()
    for m, s in vals:
        p = jnp.exp(s - mx)
        tot = p if tot is None else tot + p
        p_scr[rows, m * LANE:(m + 1) * LANE] = p.astype(BF16)
        used.add(m)
    for m in range(n_win_blocks):
        if m not in used:
            p_scr[rows, m * LANE:(m + 1) * LANE] = jnp.zeros((32, LANE), BF16)
    l_scr[rows, :] = jnp.broadcast_to(jnp.sum(tot, axis=-1, keepdims=True), (32, LANE))


def _attn_kernel(q_ref, k_ref, v_ref, kc_ref, vc_ref, bias_ref, o_ref, s_scr, p_scr, l_scr, *, n_rows):
    n_units = n_rows // ATTN_ROWS
    n_q = ATTN_ROWS * GRID_W
    n_win = ATTN_KEY_ROWS * GRID_W
    n_ctx = kc_ref.shape[0]
    n_win_blocks = n_win // LANE
    n_ctx_blocks = n_ctx // LANE
    variants = _attn_variants(n_rows)
    lane_q = lax.broadcasted_iota(jnp.int32, (n_q, LANE), 1)
    head_scale = _head_scales()
    contract_last = (((1,), (1,)), ((), ()))

    def unit(g, kind):
        geo = variants[kind]
        if kind == "top":
            kb = 0
        elif kind == "bot":
            kb = n_rows - ATTN_KEY_ROWS
        else:
            kb = g * ATTN_ROWS - WIN_H // 2
        q0 = pl.multiple_of(g * n_q, n_q)
        k0 = pl.multiple_of(kb * GRID_W, GRID_W)
        q = q_ref[pl.ds(q0, n_q), :]
        q2 = jnp.concatenate([q * head_scale[0], q * head_scale[1]], axis=0)
        kw = k_ref[pl.ds(k0, n_win), :]
        s_scr[:, 0:n_win] = lax.dot_general(q2, kw, contract_last, preferred_element_type=F32)
        s_scr[:, n_win:n_win + n_ctx] = lax.dot_general(q2, kc_ref[...], contract_last, preferred_element_type=F32)
        for hh in range(2):
            for i in range(ATTN_ROWS):
                jlo = geo["jlo"][i]
                blocks = []
                for m in range(n_win_blocks):
                    jl, jr = 2 * m, 2 * m + 1
                    vl = jlo <= jl < jlo + WIN_H
                    vr = jlo <= jr < jlo + WIN_H
                    if not (vl or vr):
                        continue
                    d = (jl - i + geo["droff"]) + WIN_H
                    blocks.append((m, d, None if (vl and vr) else ("lo" if vl else "hi")))
                for half in range(2):
                    row0 = hh * n_q + i * GRID_W + half * 32
                    _softmax_strip(s_scr, p_scr, l_scr, bias_ref, hh, row0, blocks, n_ctx_blocks, n_win_blocks)
        vw = v_ref[pl.ds(k0, n_win), :]
        o2 = jnp.dot(p_scr[:, 0:n_win], vw, preferred_element_type=F32)
        o2 = o2 + jnp.dot(p_scr[:, n_win:n_win + n_ctx], vc_ref[...], preferred_element_type=F32)
        rinv = 1.0 / l_scr[...]
        o2 = o2 * rinv
        out = jnp.where(lane_q < GRID_W, o2[0:n_q], o2[n_q:2 * n_q])
        o_ref[pl.ds(q0, n_q), :] = out.astype(o_ref.dtype)

    unit(0, "top")

    def mid_body(g, carry):
        unit(g, "mid")
        return carry

    lax.fori_loop(1, n_units - 1, mid_body, 0)
    unit(n_units - 1, "bot")


def _attention(u, u_ctx, bias):
    b, l, _ = u.shape
    n_ctx = u_ctx.shape[1]
    n_rows = l // GRID_W
    n_pairs = NA_HEADS // 2
    n_q = ATTN_ROWS * GRID_W
    n_keys = ATTN_KEY_ROWS * GRID_W + n_ctx
    qb, kb, vb = COL_Q // LANE, COL_K // LANE, COL_V // LANE
    return pl.pallas_call(
        functools.partial(_attn_kernel, n_rows=n_rows),
        out_shape=jax.ShapeDtypeStruct((b, l, NA_DIM), BF16),
        grid=(b, n_pairs),
        in_specs=[
            pl.BlockSpec((None, l, LANE), lambda i, p: (i, 0, qb + p)),
            pl.BlockSpec((None, l, LANE), lambda i, p: (i, 0, kb + p)),
            pl.BlockSpec((None, l, LANE), lambda i, p: (i, 0, vb + p)),
            pl.BlockSpec((None, n_ctx, LANE), lambda i, p: (i, 0, kb + p)),
            pl.BlockSpec((None, n_ctx, LANE), lambda i, p: (i, 0, vb + p)),
            pl.BlockSpec((2, 2 * WIN_H, GRID_W, LANE), lambda i, p: (p, 0, 0, 0)),
        ],
        out_specs=pl.BlockSpec((None, l, LANE), lambda i, p: (i, 0, p)),
        scratch_shapes=[
            pltpu.VMEM((2 * n_q, n_keys), F32),
            pltpu.VMEM((2 * n_q, n_keys), BF16),
            pltpu.VMEM((2 * n_q, LANE), F32),
        ],
        compiler_params=_params("arbitrary", "arbitrary"),
        name="natten",
    )(u, u, u, u_ctx, u_ctx, bias)


def _ctx_attn_kernel(q_ref, k_ref, v_ref, o_ref):
    n_q = q_ref.shape[0]
    lane_q = lax.broadcasted_iota(jnp.int32, (n_q, LANE), 1)
    head_scale = _head_scales()
    q = q_ref[...]
    q2 = jnp.concatenate([q * head_scale[0], q * head_scale[1]], axis=0)
    s = lax.dot_general(q2, k_ref[...], (((1,), (1,)), ((), ())), preferred_element_type=F32)
    p = jnp.exp(s - jnp.max(s, axis=-1, keepdims=True))
    rinv = 1.0 / jnp.sum(p, axis=-1, keepdims=True)
    o2 = jnp.dot(p.astype(BF16), v_ref[...], preferred_element_type=F32) * rinv
    o_ref[...] = jnp.where(lane_q < GRID_W, o2[0:n_q], o2[n_q:2 * n_q]).astype(o_ref.dtype)


def _ctx_attention(u_ctx):
    b, n_ctx, _ = u_ctx.shape
    qb, kb, vb = COL_Q // LANE, COL_K // LANE, COL_V // LANE
    return pl.pallas_call(
        _ctx_attn_kernel,
        out_shape=jax.ShapeDtypeStruct((b, n_ctx, NA_DIM), BF16),
        grid=(b, NA_HEADS // 2),
        in_specs=[
            pl.BlockSpec((None, n_ctx, LANE), lambda i, p: (i, 0, qb + p)),
            pl.BlockSpec((None, n_ctx, LANE), lambda i, p: (i, 0, kb + p)),
            pl.BlockSpec((None, n_ctx, LANE), lambda i, p: (i, 0, vb + p)),
        ],
        out_specs=pl.BlockSpec((None, n_ctx, LANE), lambda i, p: (i, 0, p)),
        compiler_params=_params("arbitrary", "arbitrary"),
        name="ctx_attn",
    )(u_ctx, u_ctx, u_ctx)


def _attn_bias_table(rpb):
    heads = rpb.shape[0]
    cols = jnp.arange(GRID_W)
    c0 = jnp.clip(cols - WIN_W // 2, 0, GRID_W - WIN_W)
    rel = cols[None, :] - cols[:, None] + (WIN_W - 1)
    inside = (cols[None, :] >= c0[:, None]) & (cols[None, :] < c0[:, None] + WIN_W)
    rel = jnp.clip(rel, 0, 2 * WIN_W - 2)
    blk = jnp.where(inside[None, None], rpb[:, :, rel], NEG_INF)
    pad = jnp.full((heads, 1, GRID_W, GRID_W), NEG_INF, F32)
    left = jnp.concatenate([pad, blk], axis=1)
    right = jnp.concatenate([blk, pad], axis=1)
    return jnp.concatenate([left, right], axis=-1).astype(F32)


def _ssd_kernel(xl_ref, zl_ref, dtl_ref, xc_ref, zc_ref, dtc_ref, cw_ref, cb_ref, dtb_ref, alog_ref,
                dvec_ref, nw_ref, cos_ref, sin_ref, yl_ref, yc_ref,
                xs_l, bc_l, xs_c, bc_c, yacc_l, yacc_c, st_ref):
    q = SSD_CHUNK
    ri = lax.broadcasted_iota(jnp.int32, (q, q), 0)
    ci = lax.broadcasted_iota(jnp.int32, (q, q), 1)
    lane = lax.broadcasted_iota(jnp.int32, (q, LANE), 1)
    lane1 = lax.broadcasted_iota(jnp.int32, (1, LANE), 1)
    rows = lax.broadcasted_iota(jnp.int32, (q, 1), 0)
    lo_half = lane < SSD_HEAD_DIM
    lo_half1 = lane1 < SSD_HEAD_DIM
    a_all = jnp.where(lane1 < 2 * SSD_HEADS, -jnp.exp(alog_ref[...]), 0.0)
    n_bc = 2 * SSD_GROUPS * SSD_STATE

    def prep(raw_ref, c, n_chunks, rope, xs_ref, bc_ref):
        base = pl.multiple_of(c * q, q)
        seq = n_chunks * q
        x = raw_ref[pl.ds(base, q), :].astype(F32)
        pstart = pl.multiple_of(jnp.maximum(base - 16, 0), 16)
        nstart = pl.multiple_of(jnp.minimum(base + q, seq - 16), 16)
        prev = raw_ref[pl.ds(pstart, 16), :][15:16, :].astype(F32) * (c > 0).astype(F32)
        nxt = raw_ref[pl.ds(nstart, 16), :][0:1, :].astype(F32) * (c < n_chunks - 1).astype(F32)
        xm1 = jnp.where(rows == 0, prev, pltpu.roll(x, 1, 0))
        xp1 = jnp.where(rows == q - 1, nxt, pltpu.roll(x, q - 1, 0))
        y = xm1 * cw_ref[0:1, :] + x * cw_ref[1:2, :] + xp1 * cw_ref[2:3, :] + cb_ref[...]
        y = _silu(y)
        xs_ref[pl.ds(base, q), :] = y[:, 0:SSD_DIM]
        for t in range(2 * SSD_GROUPS):
            blk = y[:, SSD_DIM + t * SSD_STATE:SSD_DIM + (t + 1) * SSD_STATE]
            if rope:
                sw = jnp.where(jnp.bitwise_and(lane, 63) < 32, pltpu.roll(blk, 96, 1), pltpu.roll(blk, 32, 1))
                blk = blk * cos_ref[pl.ds(base, q), :] + sw * sin_ref[pl.ds(base, q), :]
            bc_ref[pl.ds(base, q), t * SSD_STATE:(t + 1) * SSD_STATE] = blk.astype(BF16)

    def scan_chunk(c, direction, dt_ref, xs_ref, bc_ref):
        base = pl.multiple_of(c * q, q)
        dt = _softplus(dt_ref[pl.ds(base, q), :] + dtb_ref[...])
        dta = dt * a_all
        tri = jnp.where(ri >= ci, 1.0, 0.0) if direction == 0 else jnp.where(ri <= ci, 1.0, 0.0)
        acs = jnp.dot(tri.astype(F32), dta, preferred_element_type=F32, precision=HIGHEST)
        acs_t = acs.T
        mask = (ri >= ci) if direction == 0 else (ri <= ci)
        end_row = q - 1 if direction == 0 else 0
        ys = []
        for g in range(SSD_GROUPS):
            xg = xs_ref[pl.ds(base, q), g * LANE:(g + 1) * LANE]
            bg = bc_ref[pl.ds(base, q), g * SSD_STATE:(g + 1) * SSD_STATE]
            cg = bc_ref[pl.ds(base, q), (SSD_GROUPS + g) * SSD_STATE:(SSD_GROUPS + g + 1) * SSD_STATE]
            gram = lax.dot_general(cg, bg, (((1,), (1,)), ((), ())), preferred_element_type=F32)
            cols = [direction * SSD_HEADS + 2 * g, direction * SSD_HEADS + 2 * g + 1]
            a_col = [acs[:, k:k + 1] for k in cols]
            a_row = [acs_t[k:k + 1, :] for k in cols]
            d_col = [dt[:, k:k + 1] for k in cols]
            a_end = [acs[end_row:end_row + 1, k:k + 1] for k in cols]
            dtp = jnp.where(lo_half, d_col[0], d_col[1])
            xdt = (xg * dtp).astype(BF16)
            yd = []
            for k in range(2):
                decay = jnp.exp(jnp.where(mask, a_col[k] - a_row[k], NEG_INF))
                yd.append(jnp.dot((gram * decay).astype(BF16), xdt, preferred_element_type=F32))
            y_diag = jnp.where(lo_half, yd[0], yd[1])
            st = st_ref[g]
            y_off = jnp.dot(cg, st.astype(BF16), preferred_element_type=F32)
            y_off = y_off * jnp.where(lo_half, jnp.exp(a_col[0]), jnp.exp(a_col[1]))
            wp = jnp.where(lo_half, jnp.exp(a_end[0] - a_col[0]), jnp.exp(a_end[1] - a_col[1])) * dtp
            xw = (xg * wp).astype(BF16)
            bgt = bg.astype(F32).T.astype(BF16)
            upd = jnp.dot(bgt, xw, preferred_element_type=F32)
            dec = jnp.where(lo_half1, jnp.exp(a_end[0]), jnp.exp(a_end[1]))
            st_ref[g] = st * dec + upd
            ys.append(y_diag + y_off)
        return jnp.concatenate(ys, axis=-1)

    def finish(y, c, xs_ref, z_ref, out_ref):
        base = pl.multiple_of(c * q, q)
        y = y + dvec_ref[...] * xs_ref[pl.ds(base, q), :]
        y = y * _silu(z_ref[pl.ds(base, q), :].astype(F32))
        out_ref[pl.ds(base, q), :] = (_rms(y) * nw_ref[...]).astype(out_ref.dtype)

    n_c = xc_ref.shape[0] // q
    n_l = xl_ref.shape[0] // q

    def fwd_pass(raw_ref, dt_ref, n_chunks, rope, xs_ref, bc_ref, yacc_ref):
        def body(c, carry):
            prep(raw_ref, c, n_chunks, rope, xs_ref, bc_ref)
            base = pl.multiple_of(c * q, q)
            yacc_ref[pl.ds(base, q), :] = scan_chunk(c, 0, dt_ref, xs_ref, bc_ref)
            return carry

        lax.fori_loop(0, n_chunks, body, 0)

    def bwd_pass(dt_ref, n_chunks, xs_ref, bc_ref, yacc_ref, z_ref, out_ref):
        def body(t, carry):
            c = n_chunks - 1 - t
            base = pl.multiple_of(c * q, q)
            y = scan_chunk(c, 1, dt_ref, xs_ref, bc_ref) + yacc_ref[pl.ds(base, q), :]
            finish(y, c, xs_ref, z_ref, out_ref)
            return carry

        lax.fori_loop(0, n_chunks, body, 0)

    st_ref[...] = jnp.zeros_like(st_ref)
    fwd_pass(xc_ref, dtc_ref, n_c, False, xs_c, bc_c, yacc_c)
    fwd_pass(xl_ref, dtl_ref, n_l, True, xs_l, bc_l, yacc_l)
    st_ref[...] = jnp.zeros_like(st_ref)
    bwd_pass(dtc_ref, n_c, xs_c, bc_c, yacc_c, zc_ref, yc_ref)
    bwd_pass(dtl_ref, n_l, xs_l, bc_l, yacc_l, zl_ref, yl_ref)


def _ssd(u, dt, u_ctx, dt_ctx, conv_w, conv_b, dt_bias, a_log, d_vec, norm_w, cos_t, sin_t):
    b, l, _ = u.shape
    n_ctx = u_ctx.shape[1]
    xb = COL_XBC // XBC_DIM
    zb = COL_ZSS // SSD_DIM
    n_bc = 2 * SSD_GROUPS * SSD_STATE
    const = lambda shape: pl.BlockSpec(shape, lambda i: (0,) * len(shape))
    return pl.pallas_call(
        _ssd_kernel,
        out_shape=(jax.ShapeDtypeStruct((b, l, SSD_DIM), BF16), jax.ShapeDtypeStruct((b, n_ctx, SSD_DIM), BF16)),
        grid=(b,),
        in_specs=[
            pl.BlockSpec((None, l, XBC_DIM), lambda i: (i, 0, xb)),
            pl.BlockSpec((None, l, SSD_DIM), lambda i: (i, 0, zb)),
            pl.BlockSpec((None, l, DT_PAD), lambda i: (i, 0, 0)),
            pl.BlockSpec((None, n_ctx, XBC_DIM), lambda i: (i, 0, xb)),
            pl.BlockSpec((None, n_ctx, SSD_DIM), lambda i: (i, 0, zb)),
            pl.BlockSpec((None, n_ctx, DT_PAD), lambda i: (i, 0, 0)),
            const((3, XBC_DIM)),
            const((1, XBC_DIM)),
            const((1, DT_PAD)),
            const((1, DT_PAD)),
            const((1, SSD_DIM)),
            const((1, SSD_DIM)),
            const((l, SSD_STATE)),
            const((l, SSD_STATE)),
        ],
        out_specs=(
            pl.BlockSpec((None, l, SSD_DIM), lambda i: (i, 0, 0)),
            pl.BlockSpec((None, n_ctx, SSD_DIM), lambda i: (i, 0, 0)),
        ),
        scratch_shapes=[
            pltpu.VMEM((l, SSD_DIM), F32),
            pltpu.VMEM((l, n_bc), BF16),
            pltpu.VMEM((n_ctx, SSD_DIM), F32),
            pltpu.VMEM((n_ctx, n_bc), BF16),
            pltpu.VMEM((l, SSD_DIM), F32),
            pltpu.VMEM((n_ctx, SSD_DIM), F32),
            pltpu.VMEM((SSD_GROUPS, SSD_STATE, LANE), F32),
        ],
        compiler_params=_params("arbitrary"),
        name="ssd",
    )(u, u, dt, u_ctx, u_ctx, dt_ctx, conv_w, conv_b, dt_bias, a_log, d_vec, norm_w, cos_t, sin_t)


def _rope_tables(seq):
    n_freq = SSD_STATE // 4
    pos = jnp.arange(seq)
    row_pos = (pos // GRID_W).astype(F32)
    col_pos = (pos % GRID_W).astype(F32)
    inv_freq = ROPE_BASE ** (-jnp.arange(n_freq, dtype=F32) / n_freq)
    ar = row_pos[:, None] * inv_freq
    ac = col_pos[:, None] * inv_freq
    cos_t = jnp.concatenate([jnp.cos(ar), jnp.cos(ar), jnp.cos(ac), jnp.cos(ac)], axis=-1)
    sin_t = jnp.concatenate([-jnp.sin(ar), jnp.sin(ar), -jnp.sin(ac), jnp.sin(ac)], axis=-1)
    return cos_t, sin_t


def _outproj_kernel(ua_ref, up_ref, un_ref, yna_ref, zna_ref, yss_ref, x_ref, gate_ref, gpost_ref, cw_ref,
                    w_ref, o_ref):
    j = pl.program_id(1)
    nt = pl.num_programs(1)
    tm = ua_ref.shape[0]
    cd = CONV_DIM
    rows = lax.broadcasted_iota(jnp.int32, (tm, 1), 0)
    ua = ua_ref[...]
    t = ua[:, 2 * cd:3 * cd].astype(F32) * ua[:, 0:cd].astype(F32)
    up = up_ref[15:16, :]
    un = un_ref[0:1, :]
    tprev = up[:, 2 * cd:3 * cd].astype(F32) * up[:, 0:cd].astype(F32) * (j > 0).astype(F32)
    tnext = un[:, 2 * cd:3 * cd].astype(F32) * un[:, 0:cd].astype(F32) * (j < nt - 1).astype(F32)
    tm1 = jnp.where(rows == 0, tprev, pltpu.roll(t, 1, 0))
    tp1 = jnp.where(rows == tm - 1, tnext, pltpu.roll(t, tm - 1, 0))
    conv = tm1 * cw_ref[0:1, :] + t * cw_ref[1:2, :] + tp1 * cw_ref[2:3, :]
    ysc = ua[:, cd:2 * cd].astype(F32) * conv * _silu(ua[:, 3 * cd:4 * cd].astype(F32))
    yna = yna_ref[...].astype(F32) * _silu(zna_ref[...].astype(F32))
    ycat = jnp.concatenate([ysc.astype(BF16), yna.astype(BF16), yss_ref[...]], axis=-1)
    out = jnp.dot(ycat, w_ref[...], preferred_element_type=F32)
    o_ref[...] = x_ref[...] + gate_ref[...] * (_rms(out) * gpost_ref[...])


def _outproj(u, y_na, y_ss, x, gate, g_post, conv_w, w_out, tm):
    b, l, d = x.shape
    hb = tm // 16
    n_hb = l // 16
    zb = COL_ZNA // NA_DIM
    return pl.pallas_call(
        _outproj_kernel,
        out_shape=jax.ShapeDtypeStruct((b, l, d), F32),
        grid=(b, l // tm),
        in_specs=[
            pl.BlockSpec((None, tm, 4 * CONV_DIM), lambda i, j: (i, j, 0)),
            pl.BlockSpec((None, 16, 4 * CONV_DIM), lambda i, j: (i, jnp.maximum(j * hb - 1, 0), 0)),
            pl.BlockSpec((None, 16, 4 * CONV_DIM), lambda i, j: (i, jnp.minimum((j + 1) * hb, n_hb - 1), 0)),
            pl.BlockSpec((None, tm, NA_DIM), lambda i, j: (i, j, 0)),
            pl.BlockSpec((None, tm, NA_DIM), lambda i, j: (i, j, zb)),
            pl.BlockSpec((None, tm, SSD_DIM), lambda i, j: (i, j, 0)),
            pl.BlockSpec((None, tm, d), lambda i, j: (i, j, 0)),
            pl.BlockSpec((None, 1, d), lambda i, j: (i, 0, 0)),
            pl.BlockSpec((1, d), lambda i, j: (0, 0)),
            pl.BlockSpec((3, CONV_DIM), lambda i, j: (0, 0)),
            pl.BlockSpec((d, d), lambda i, j: (0, 0)),
        ],
        out_specs=pl.BlockSpec((None, tm, d), lambda i, j: (i, j, 0)),
        compiler_params=_params("arbitrary", "arbitrary"),
        name="outproj",
    )(u, u, u, y_na, u, y_ss, x, gate, g_post, conv_w, w_out)


def kernel(x, c, ctx, c_ctx, w_ada, b_ada, g_pre, g_post, w_in, conv_a_w, rpb, ssd_conv_w, ssd_conv_b,
           dt_bias_f, dt_bias_b, a_log_f, a_log_b, ssd_d, ssd_norm_w, w_out):
    depth = w_ada.shape[0]
    bsz, seq, d = x.shape
    n_ctx = ctx.shape[1]
    assert d == D_MODEL and seq % (ATTN_ROWS * GRID_W) == 0 and seq // GRID_W >= ATTN_KEY_ROWS
    assert n_ctx % SSD_CHUNK == 0 and w_in.shape[-1] == U_DIM + 2 * SSD_HEADS
    tm_lat = min(512, seq)
    tm_ctx = n_ctx

    n_rows = -(-(bsz + 1) // 8) * 8
    cc = jnp.zeros((n_rows, d), F32).at[:bsz].set(c).at[bsz].set(c_ctx)
    mods = _ada_all_layers(cc, w_ada, b_ada)
    cos_t, sin_t = _rope_tables(seq)
    pad8 = DT_PAD - 2 * SSD_HEADS

    x_ctx = ctx
    for layer in range(depth):
        m = mods[layer]
        shift, scale, gate = (m[:bsz, k * d:(k + 1) * d].reshape(bsz, 1, d) for k in range(3))
        shift_c, scale_c, gate_c = (jnp.broadcast_to(m[bsz, k * d:(k + 1) * d], (bsz, 1, d)) for k in range(3))
        w_u = w_in[layer, :, :U_DIM].astype(BF16)
        w_dt = jnp.pad(w_in[layer, :, U_DIM:], ((0, 0), (0, pad8))).astype(BF16)
        gpre = g_pre[layer].reshape(1, d)
        gpost = g_post[layer].reshape(1, d)
        dt_bias = jnp.pad(jnp.concatenate([dt_bias_f[layer], dt_bias_b[layer]]), (0, pad8)).reshape(1, DT_PAD)
        a_log = jnp.pad(jnp.concatenate([a_log_f[layer], a_log_b[layer]]), (0, pad8)).reshape(1, DT_PAD)
        d_vec = jnp.repeat(ssd_d[layer], SSD_HEAD_DIM).reshape(1, SSD_DIM)
        bias = _attn_bias_table(rpb[layer])

        u_lat, dt_lat = _inproj(x, shift, scale, gpre, w_u, w_dt, tm_lat)
        u_ctx, dt_ctx = _inproj(x_ctx, shift_c, scale_c, gpre, w_u, w_dt, tm_ctx)
        y_na = _attention(u_lat, u_ctx, bias)
        y_ss, y_ss_ctx = _ssd(u_lat, dt_lat, u_ctx, dt_ctx, ssd_conv_w[layer], ssd_conv_b[layer].reshape(1, -1),
                              dt_bias, a_log, d_vec, ssd_norm_w[layer].reshape(1, -1), cos_t, sin_t)
        w_o = w_out[layer].astype(BF16)
        x = _outproj(u_lat, y_na, y_ss, x, gate, gpost, conv_a_w[layer], w_o, tm_lat)
        if layer < depth - 1:
            y_na_ctx = _ctx_attention(u_ctx)
            x_ctx = _outproj(u_ctx, y_na_ctx, y_ss_ctx, x_ctx, gate_c, gpost, conv_a_w[layer], w_o, tm_ctx)
    return x
```

```python
import functools
import math

import jax
import jax.numpy as jnp
import numpy as np
from jax import lax
from jax.experimental import pallas as pl
from jax.experimental.pallas import tpu as pltpu

F32 = jnp.float32
BF16 = jnp.bfloat16
HIGHEST = lax.Precision.HIGHEST

D_MODEL = 1024
GRID_W = 64
EPS = 1e-6
CONV_DIM = 256
NA_HEADS = 8
NA_HEAD_DIM = 64
NA_DIM = NA_HEADS * NA_HEAD_DIM
WIN_H = 8
WIN_W = 16
SSD_HEADS = 4
SSD_HEAD_DIM = 64
SSD_DIM = SSD_HEADS * SSD_HEAD_DIM
SSD_GROUPS = 2
SSD_STATE = 128
SSD_CHUNK = 128
XBC_DIM = SSD_DIM + 2 * SSD_GROUPS * SSD_STATE
ROPE_BASE = 10000.0
U_DIM = 4 * CONV_DIM + 4 * NA_DIM + XBC_DIM + SSD_DIM
DT_PAD = 128
COL_Q = 4 * CONV_DIM
COL_K = COL_Q + NA_DIM
COL_V = COL_K + NA_DIM
COL_ZNA = COL_V + NA_DIM
COL_XBC = COL_ZNA + NA_DIM
COL_ZSS = COL_XBC + XBC_DIM

LANE = 128
ATTN_ROWS = 4
ATTN_KEY_ROWS = 12
VMEM_LIMIT = 56 * 1024 * 1024

NEG_INF = float("-inf")


def _sigmoid(x):
    return 1.0 / (1.0 + jnp.exp(-x))


def _silu(x):
    return x * _sigmoid(x)


def _softplus(x):
    return jnp.maximum(x, 0.0) + jnp.log1p(jnp.exp(-jnp.abs(x)))


def _rms(x):
    return x * lax.rsqrt(jnp.mean(x * x, axis=-1, keepdims=True) + EPS)


def _params(*sem):
    return pltpu.CompilerParams(dimension_semantics=sem, vmem_limit_bytes=VMEM_LIMIT)


def _ada_kernel(cc_ref, w_ref, b_ref, o_ref):
    s = _silu(cc_ref[...])
    o_ref[...] = jnp.dot(s, w_ref[...], preferred_element_type=F32, precision=HIGHEST) + b_ref[...]


def _ada_all_layers(cc, w_ada, b_ada):
    depth = w_ada.shape[0]
    r = cc.shape[0]
    nblk = 3 * D_MODEL // D_MODEL
    return pl.pallas_call(
        _ada_kernel,
        out_shape=jax.ShapeDtypeStruct((depth, r, 3 * D_MODEL), F32),
        grid=(depth, nblk),
        in_specs=[
            pl.BlockSpec((r, D_MODEL), lambda l, n: (0, 0)),
            pl.BlockSpec((None, D_MODEL, D_MODEL), lambda l, n: (l, 0, n)),
            pl.BlockSpec((None, 1, D_MODEL), lambda l, n: (l, 0, n)),
        ],
        out_specs=pl.BlockSpec((None, r, D_MODEL), lambda l, n: (l, 0, n)),
        compiler_params=_params("arbitrary", "arbitrary"),
        name="ada",
    )(cc, w_ada, b_ada.reshape(depth, 1, 3 * D_MODEL))


def _inproj_kernel(x_ref, shift_ref, scale_ref, g_ref, w_ref, wdt_ref, u_ref, dt_ref, dt_scr):
    h = _rms(x_ref[...]) * g_ref[...]
    h = h * (1.0 + scale_ref[...]) + shift_ref[...]
    hb = h.astype(BF16)
    nb = 512
    for n in range(0, U_DIM, nb):
        u_ref[:, n:n + nb] = jnp.dot(hb, w_ref[:, n:n + nb], preferred_element_type=F32).astype(BF16)
    dt_scr[...] = jnp.dot(hb, wdt_ref[...], preferred_element_type=F32)
    dt_ref[...] = dt_scr[...].T[0:2 * SSD_HEADS, :]


def _inproj(x, shift, scale, g_pre, w_u, w_dt, tm):
    b, l, d = x.shape
    return pl.pallas_call(
        _inproj_kernel,
        out_shape=(jax.ShapeDtypeStruct((b, l, U_DIM), BF16), jax.ShapeDtypeStruct((b, 2 * SSD_HEADS, l), F32)),
        grid=(b, l // tm),
        in_specs=[
            pl.BlockSpec((None, tm, d), lambda i, j: (i, j, 0)),
            pl.BlockSpec((None, 1, d), lambda i, j: (i, 0, 0)),
            pl.BlockSpec((None, 1, d), lambda i, j: (i, 0, 0)),
            pl.BlockSpec((1, d), lambda i, j: (0, 0)),
            pl.BlockSpec((d, U_DIM), lambda i, j: (0, 0)),
            pl.BlockSpec((d, DT_PAD), lambda i, j: (0, 0)),
        ],
        out_specs=(
            pl.BlockSpec((None, tm, U_DIM), lambda i, j: (i, j, 0)),
            pl.BlockSpec((None, 2 * SSD_HEADS, tm), lambda i, j: (i, 0, j)),
        ),
        scratch_shapes=[pltpu.VMEM((tm, DT_PAD), F32)],
        compiler_params=_params("arbitrary", "arbitrary"),
        name="inproj",
    )(x, shift, scale, g_pre, w_u, w_dt)


def _attn_variants(n_rows):
    return {
        "top": dict(jlo=[0, 0, 0, 0], droff=0),
        "mid": dict(jlo=[0, 1, 2, 3], droff=-WIN_H // 2),
        "bot": dict(jlo=[4, 4, 4, 4], droff=-WIN_H),
    }


def _head_scales():
    lane1 = lax.broadcasted_iota(jnp.int32, (1, LANE), 1)
    scale = NA_HEAD_DIM ** -0.5
    lo = jnp.where(lane1 < NA_HEAD_DIM, scale, 0.0).astype(BF16)
    hi = jnp.where(lane1 >= NA_HEAD_DIM, scale, 0.0).astype(BF16)
    return lo, hi


def _softmax_strip(s_scr, p_scr, l_scr, bias_ref, hh, row0, blocks, n_ctx_blocks, n_win_blocks):
    rows = slice(row0, row0 + 32)
    brow = (row0 % GRID_W)
    lane = lax.broadcasted_iota(jnp.int32, (32, LANE), 1)
    vals = []
    for m, d, keep in blocks:
        s = s_scr[rows, m * LANE:(m + 1) * LANE] + bias_ref[hh, d, brow:brow + 32, :]
        if keep == "lo":
            s = jnp.where(lane < GRID_W, s, NEG_INF)
        elif keep == "hi":
            s = jnp.where(lane >= GRID_W, s, NEG_INF)
        vals.append((m, s))
    for c in range(n_ctx_blocks):
        m = n_win_blocks + c
        vals.append((m, s_scr[rows, m * LANE:(m + 1) * LANE]))
    mx = vals[0][1]
    for _, s in vals[1:]:
        mx = jnp.maximum(mx, s)
    mx = jnp.max(mx, axis=-1, keepdims=True)
    tot = None
    used = set()
    for m, s in vals:
        p = jnp.exp(s - mx)
        tot = p if tot is None else tot + p
        p_scr[rows, m * LANE:(m + 1) * LANE] = p.astype(BF16)
        used.add(m)
    for m in range(n_win_blocks):
        if m not in used:
            p_scr[rows, m * LANE:(m + 1) * LANE] = jnp.zeros((32, LANE), BF16)
    l_scr[rows, :] = jnp.broadcast_to(jnp.sum(tot, axis=-1, keepdims=True), (32, LANE))


def _attn_kernel(q_ref, k_ref, v_ref, kc_ref, vc_ref, bias_ref, o_ref, s0, s1, p0, p1, l0, l1, *, n_rows):
    n_units = n_rows // ATTN_ROWS
    n_q = ATTN_ROWS * GRID_W
    n_win = ATTN_KEY_ROWS * GRID_W
    n_ctx = kc_ref.shape[0]
    n_win_blocks = n_win // LANE
    n_ctx_blocks = n_ctx // LANE
    variants = _attn_variants(n_rows)
    lane_q = lax.broadcasted_iota(jnp.int32, (n_q, LANE), 1)
    head_scale = _head_scales()
    contract_last = (((1,), (1,)), ((), ()))
    s_bufs, p_bufs, l_bufs = (s0, s1), (p0, p1), (l0, l1)

    def key_start(g):
        kb = jnp.clip(g * ATTN_ROWS - WIN_H // 2, 0, n_rows - ATTN_KEY_ROWS)
        return pl.multiple_of(kb * GRID_W, GRID_W)

    def scores(g, slot):
        s_scr = s_bufs[slot]
        q = q_ref[pl.ds(pl.multiple_of(g * n_q, n_q), n_q), :]
        q2 = jnp.concatenate([q * head_scale[0], q * head_scale[1]], axis=0)
        kw = k_ref[pl.ds(key_start(g), n_win), :]
        s_scr[:, 0:n_win] = lax.dot_general(q2, kw, contract_last, preferred_element_type=F32)
        s_scr[:, n_win:n_win + n_ctx] = lax.dot_general(q2, kc_ref[...], contract_last, preferred_element_type=F32)

    def softmax(kind, slot):
        geo = variants[kind]
        for hh in range(2):
            for i in range(ATTN_ROWS):
                jlo = geo["jlo"][i]
                blocks = []
                for m in range(n_win_blocks):
                    jl, jr = 2 * m, 2 * m + 1
                    vl = jlo <= jl < jlo + WIN_H
                    vr = jlo <= jr < jlo + WIN_H
                    if not (vl or vr):
                        continue
                    d = (jl - i + geo["droff"]) + WIN_H
                    blocks.append((m, d, None if (vl and vr) else ("lo" if vl else "hi")))
                for half in range(2):
                    row0 = hh * n_q + i * GRID_W + half * 32
                    _softmax_strip(s_bufs[slot], p_bufs[slot], l_bufs[slot], bias_ref, hh, row0, blocks,
                                   n_ctx_blocks, n_win_blocks)

    def pv(g, slot):
        p_scr = p_bufs[slot]
        vw = v_ref[pl.ds(key_start(g), n_win), :]
        o2 = jnp.dot(p_scr[:, 0:n_win], vw, preferred_element_type=F32)
        o2 = o2 + jnp.dot(p_scr[:, n_win:n_win + n_ctx], vc_ref[...], preferred_element_type=F32)
        o2 = o2 * (1.0 / l_bufs[slot][...])
        out = jnp.where(lane_q < GRID_W, o2[0:n_q], o2[n_q:2 * n_q])
        o_ref[pl.ds(pl.multiple_of(g * n_q, n_q), n_q), :] = out.astype(o_ref.dtype)

    scores(jnp.int32(0), 0)
    scores(jnp.int32(1), 1)
    softmax("top", 0)

    def pair_body(tt, carry):
        t = 1 + 2 * tt
        scores(t + 1, 0)
        softmax("mid", 1)
        pv(t - 1, 0)
        scores(t + 2, 1)
        softmax("mid", 0)
        pv(t, 1)
        return carry

    lax.fori_loop(0, (n_units - 2) // 2, pair_body, 0)
    softmax("bot", 1)
    pv(jnp.int32(n_units - 2), 0)
    pv(jnp.int32(n_units - 1), 1)


def _attention(u, u_ctx, bias):
    b, l, _ = u.shape
    n_ctx = u_ctx.shape[1]
    n_rows = l // GRID_W
    n_pairs = NA_HEADS // 2
    n_q = ATTN_ROWS * GRID_W
    n_keys = ATTN_KEY_ROWS * GRID_W + n_ctx
    qb, kb, vb = COL_Q // LANE, COL_K // LANE, COL_V // LANE
    return pl.pallas_call(
        functools.partial(_attn_kernel, n_rows=n_rows),
        out_shape=jax.ShapeDtypeStruct((b, l, NA_DIM), BF16),
        grid=(b, n_pairs),
        in_specs=[
            pl.BlockSpec((None, l, LANE), lambda i, p: (i, 0, qb + p)),
            pl.BlockSpec((None, l, LANE), lambda i, p: (i, 0, kb + p)),
            pl.BlockSpec((None, l, LANE), lambda i, p: (i, 0, vb + p)),
            pl.BlockSpec((None, n_ctx, LANE), lambda i, p: (i, 0, kb + p)),
            pl.BlockSpec((None, n_ctx, LANE), lambda i, p: (i, 0, vb + p)),
            pl.BlockSpec((2, 2 * WIN_H, GRID_W, LANE), lambda i, p: (p, 0, 0, 0)),
        ],
        out_specs=pl.BlockSpec((None, l, LANE), lambda i, p: (i, 0, p)),
        scratch_shapes=[
            pltpu.VMEM((2 * n_q, n_keys), F32),
            pltpu.VMEM((2 * n_q, n_keys), F32),
            pltpu.VMEM((2 * n_q, n_keys), BF16),
            pltpu.VMEM((2 * n_q, n_keys), BF16),
            pltpu.VMEM((2 * n_q, LANE), F32),
            pltpu.VMEM((2 * n_q, LANE), F32),
        ],
        compiler_params=_params("arbitrary", "arbitrary"),
        name="natten",
    )(u, u, u, u_ctx, u_ctx, bias)


def _ctx_attn_kernel(q_ref, k_ref, v_ref, o_ref):
    n_q = q_ref.shape[0]
    lane_q = lax.broadcasted_iota(jnp.int32, (n_q, LANE), 1)
    head_scale = _head_scales()
    q = q_ref[...]
    q2 = jnp.concatenate([q * head_scale[0], q * head_scale[1]], axis=0)
    s = lax.dot_general(q2, k_ref[...], (((1,), (1,)), ((), ())), preferred_element_type=F32)
    p = jnp.exp(s - jnp.max(s, axis=-1, keepdims=True))
    rinv = 1.0 / jnp.sum(p, axis=-1, keepdims=True)
    o2 = jnp.dot(p.astype(BF16), v_ref[...], preferred_element_type=F32) * rinv
    o_ref[...] = jnp.where(lane_q < GRID_W, o2[0:n_q], o2[n_q:2 * n_q]).astype(o_ref.dtype)


def _ctx_attention(u_ctx):
    b, n_ctx, _ = u_ctx.shape
    qb, kb, vb = COL_Q // LANE, COL_K // LANE, COL_V // LANE
    return pl.pallas_call(
        _ctx_attn_kernel,
        out_shape=jax.ShapeDtypeStruct((b, n_ctx, NA_DIM), BF16),
        grid=(b, NA_HEADS // 2),
        in_specs=[
            pl.BlockSpec((None, n_ctx, LANE), lambda i, p: (i, 0, qb + p)),
            pl.BlockSpec((None, n_ctx, LANE), lambda i, p: (i, 0, kb + p)),
            pl.BlockSpec((None, n_ctx, LANE), lambda i, p: (i, 0, vb + p)),
        ],
        out_specs=pl.BlockSpec((None, n_ctx, LANE), lambda i, p: (i, 0, p)),
        compiler_params=_params("arbitrary", "arbitrary"),
        name="ctx_attn",
    )(u_ctx, u_ctx, u_ctx)


def _attn_bias_table(rpb):
    heads = rpb.shape[0]
    cols = jnp.arange(GRID_W)
    c0 = jnp.clip(cols - WIN_W // 2, 0, GRID_W - WIN_W)
    rel = cols[None, :] - cols[:, None] + (WIN_W - 1)
    inside = (cols[None, :] >= c0[:, None]) & (cols[None, :] < c0[:, None] + WIN_W)
    rel = jnp.clip(rel, 0, 2 * WIN_W - 2)
    blk = jnp.where(inside[None, None], rpb[:, :, rel], NEG_INF)
    pad = jnp.full((heads, 1, GRID_W, GRID_W), NEG_INF, F32)
    left = jnp.concatenate([pad, blk], axis=1)
    right = jnp.concatenate([blk, pad], axis=1)
    return jnp.concatenate([left, right], axis=-1).astype(F32)


def _ssd_kernel(xl_ref, zl_ref, dtl_ref, xc_ref, zc_ref, dtc_ref, cw_ref, cb_ref, dtb_ref, alog_ref,
                dvec_ref, nw_ref, cos_ref, sin_ref, yl_ref, yc_ref,
                xs_l, bc_l, xs_c, bc_c, yacc_l, yacc_c, st_ref):
    q = SSD_CHUNK
    ri = lax.broadcasted_iota(jnp.int32, (q, q), 0)
    ci = lax.broadcasted_iota(jnp.int32, (q, q), 1)
    lane = lax.broadcasted_iota(jnp.int32, (q, LANE), 1)
    lane1 = lax.broadcasted_iota(jnp.int32, (1, LANE), 1)
    rows8 = lax.broadcasted_iota(jnp.int32, (8, 1), 0)
    lo_half = lane < SSD_HEAD_DIM
    lo_half1 = lane1 < SSD_HEAD_DIM
    a_all = -jnp.exp(alog_ref[...])
    n_hd = 2 * SSD_HEADS

    def prep(raw_ref, c, n_chunks, rope, xs_ref, bc_ref):
        base = pl.multiple_of(c * q, q)
        seq = n_chunks * q
        x = raw_ref[pl.ds(base, q), :].astype(F32)
        pstart = pl.multiple_of(jnp.maximum(base - 16, 0), 16)
        nstart = pl.multiple_of(jnp.minimum(base + q, seq - 16), 16)
        prev = raw_ref[pl.ds(pstart, 16), :][15:16, :].astype(F32) * (c > 0).astype(F32)
        nxt = raw_ref[pl.ds(nstart, 16), :][0:1, :].astype(F32) * (c < n_chunks - 1).astype(F32)
        xm1 = pltpu.roll(x, 1, 0)
        xm1 = jnp.concatenate([jnp.where(rows8 == 0, prev, xm1[0:8]), xm1[8:]], axis=0)
        xp1 = pltpu.roll(x, q - 1, 0)
        xp1 = jnp.concatenate([xp1[:q - 8], jnp.where(rows8 == 7, nxt, xp1[q - 8:])], axis=0)
        y = xm1 * cw_ref[0:1, :] + x * cw_ref[1:2, :] + xp1 * cw_ref[2:3, :] + cb_ref[...]
        y = _silu(y)
        xs_ref[pl.ds(base, q), :] = y[:, 0:SSD_DIM]
        for t in range(2 * SSD_GROUPS):
            blk = y[:, SSD_DIM + t * SSD_STATE:SSD_DIM + (t + 1) * SSD_STATE]
            if rope:
                sw = jnp.where(jnp.bitwise_and(lane, 63) < 32, pltpu.roll(blk, 96, 1), pltpu.roll(blk, 32, 1))
                blk = blk * cos_ref[pl.ds(base, q), :] + sw * sin_ref[pl.ds(base, q), :]
            bc_ref[pl.ds(base, q), t * SSD_STATE:(t + 1) * SSD_STATE] = blk.astype(BF16)

    def scan_chunk(c, direction, dt_ref, xs_ref, bc_ref):
        base = pl.multiple_of(c * q, q)
        dt_t = _softplus(dt_ref[:, pl.ds(base, q)] + dtb_ref[...])
        tri = jnp.where(ri <= ci, 1.0, 0.0) if direction == 0 else jnp.where(ri >= ci, 1.0, 0.0)
        acs_t = jnp.dot(dt_t * a_all, tri.astype(F32), preferred_element_type=F32, precision=HIGHEST)
        stacked = jnp.concatenate([dt_t, acs_t, jnp.zeros((q - 2 * n_hd, q), F32)], axis=0)
        cols_all = stacked.T
        mask = (ri >= ci) if direction == 0 else (ri <= ci)
        end = q - 1 if direction == 0 else 0
        ys = []
        for g in range(SSD_GROUPS):
            xg = xs_ref[pl.ds(base, q), g * LANE:(g + 1) * LANE]
            bg = bc_ref[pl.ds(base, q), g * SSD_STATE:(g + 1) * SSD_STATE]
            cg = bc_ref[pl.ds(base, q), (SSD_GROUPS + g) * SSD_STATE:(SSD_GROUPS + g + 1) * SSD_STATE]
            gram = lax.dot_general(cg, bg, (((1,), (1,)), ((), ())), preferred_element_type=F32)
            heads = [direction * SSD_HEADS + 2 * g, direction * SSD_HEADS + 2 * g + 1]
            a_bc = [jnp.broadcast_to(cols_all[:, n_hd + k:n_hd + k + 1], (q, LANE)) for k in heads]
            d_bc = [jnp.broadcast_to(cols_all[:, k:k + 1], (q, LANE)) for k in heads]
            a_row = [acs_t[k:k + 1, :] for k in heads]
            a_end = [acs_t[k:k + 1, end:end + 1] for k in heads]
            dtp = jnp.where(lo_half, d_bc[0], d_bc[1])
            acp = jnp.where(lo_half, a_bc[0], a_bc[1])
            a_end_p = jnp.where(lo_half1, a_end[0], a_end[1])
            xdt = (xg * dtp).astype(BF16)
            yd = []
            for k in range(2):
                decay = jnp.exp(jnp.where(mask, a_bc[k] - a_row[k], NEG_INF))
                yd.append(jnp.dot((gram * decay).astype(BF16), xdt, preferred_element_type=F32))
            y_diag = jnp.where(lo_half, yd[0], yd[1])
            st = st_ref[g]
            y_off = jnp.dot(cg, st.astype(BF16), preferred_element_type=F32) * jnp.exp(acp)
            xw = (xg * (jnp.exp(a_end_p - acp) * dtp)).astype(BF16)
            bgt = bg.astype(F32).T.astype(BF16)
            upd = jnp.dot(bgt, xw, preferred_element_type=F32)
            st_ref[g] = st * jnp.exp(a_end_p) + upd
            ys.append(y_diag + y_off)
        return jnp.concatenate(ys, axis=-1)

    def finish(y, c, xs_ref, z_ref, out_ref):
        base = pl.multiple_of(c * q, q)
        y = y + dvec_ref[...] * xs_ref[pl.ds(base, q), :]
        y = y * _silu(z_ref[pl.ds(base, q), :].astype(F32))
        out_ref[pl.ds(base, q), :] = (_rms(y) * nw_ref[...]).astype(out_ref.dtype)

    n_c = xc_ref.shape[0] // q
    n_l = xl_ref.shape[0] // q

    def fwd_pass(raw_ref, dt_ref, n_chunks, rope, xs_ref, bc_ref, yacc_ref):
        def body(c, carry):
            prep(raw_ref, c, n_chunks, rope, xs_ref, bc_ref)
            base = pl.multiple_of(c * q, q)
            yacc_ref[pl.ds(base, q), :] = scan_chunk(c, 0, dt_ref, xs_ref, bc_ref)
            return carry

        lax.fori_loop(0, n_chunks, body, 0, unroll=2)

    def bwd_pass(dt_ref, n_chunks, xs_ref, bc_ref, yacc_ref, z_ref, out_ref):
        def body(t, carry):
            c = n_chunks - 1 - t
            base = pl.multiple_of(c * q, q)
            y = scan_chunk(c, 1, dt_ref, xs_ref, bc_ref) + yacc_ref[pl.ds(base, q), :]
            finish(y, c, xs_ref, z_ref, out_ref)
            return carry

        lax.fori_loop(0, n_chunks, body, 0, unroll=2)

    st_ref[...] = jnp.zeros_like(st_ref)
    fwd_pass(xc_ref, dtc_ref, n_c, False, xs_c, bc_c, yacc_c)
    fwd_pass(xl_ref, dtl_ref, n_l, True, xs_l, bc_l, yacc_l)
    st_ref[...] = jnp.zeros_like(st_ref)
    bwd_pass(dtc_ref, n_c, xs_c, bc_c, yacc_c, zc_ref, yc_ref)
    bwd_pass(dtl_ref, n_l, xs_l, bc_l, yacc_l, zl_ref, yl_ref)


def _ssd(u, dt, u_ctx, dt_ctx, conv_w, conv_b, dt_bias, a_log, d_vec, norm_w, cos_t, sin_t):
    b, l, _ = u.shape
    n_ctx = u_ctx.shape[1]
    xb = COL_XBC // XBC_DIM
    zb = COL_ZSS // SSD_DIM
    n_bc = 2 * SSD_GROUPS * SSD_STATE
    const = lambda shape: pl.BlockSpec(shape, lambda i: (0,) * len(shape))
    return pl.pallas_call(
        _ssd_kernel,
        out_shape=(jax.ShapeDtypeStruct((b, l, SSD_DIM), BF16), jax.ShapeDtypeStruct((b, n_ctx, SSD_DIM), BF16)),
        grid=(b,),
        in_specs=[
            pl.BlockSpec((None, l, XBC_DIM), lambda i: (i, 0, xb)),
            pl.BlockSpec((None, l, SSD_DIM), lambda i: (i, 0, zb)),
            pl.BlockSpec((None, 2 * SSD_HEADS, l), lambda i: (i, 0, 0)),
            pl.BlockSpec((None, n_ctx, XBC_DIM), lambda i: (i, 0, xb)),
            pl.BlockSpec((None, n_ctx, SSD_DIM), lambda i: (i, 0, zb)),
            pl.BlockSpec((None, 2 * SSD_HEADS, n_ctx), lambda i: (i, 0, 0)),
            const((3, XBC_DIM)),
            const((1, XBC_DIM)),
            const((2 * SSD_HEADS, SSD_CHUNK)),
            const((2 * SSD_HEADS, SSD_CHUNK)),
            const((1, SSD_DIM)),
            const((1, SSD_DIM)),
            const((l, SSD_STATE)),
            const((l, SSD_STATE)),
        ],
        out_specs=(
            pl.BlockSpec((None, l, SSD_DIM), lambda i: (i, 0, 0)),
            pl.BlockSpec((None, n_ctx, SSD_DIM), lambda i: (i, 0, 0)),
        ),
        scratch_shapes=[
            pltpu.VMEM((l, SSD_DIM), F32),
            pltpu.VMEM((l, n_bc), BF16),
            pltpu.VMEM((n_ctx, SSD_DIM), F32),
            pltpu.VMEM((n_ctx, n_bc), BF16),
            pltpu.VMEM((l, SSD_DIM), F32),
            pltpu.VMEM((n_ctx, SSD_DIM), F32),
            pltpu.VMEM((SSD_GROUPS, SSD_STATE, LANE), F32),
        ],
        compiler_params=_params("arbitrary"),
        name="ssd",
    )(u, u, dt, u_ctx, u_ctx, dt_ctx, conv_w, conv_b, dt_bias, a_log, d_vec, norm_w, cos_t, sin_t)


def _rope_tables(seq):
    n_freq = SSD_STATE // 4
    pos = jnp.arange(seq)
    row_pos = (pos // GRID_W).astype(F32)
    col_pos = (pos % GRID_W).astype(F32)
    inv_freq = ROPE_BASE ** (-jnp.arange(n_freq, dtype=F32) / n_freq)
    ar = row_pos[:, None] * inv_freq
    ac = col_pos[:, None] * inv_freq
    cos_t = jnp.concatenate([jnp.cos(ar), jnp.cos(ar), jnp.cos(ac), jnp.cos(ac)], axis=-1)
    sin_t = jnp.concatenate([-jnp.sin(ar), jnp.sin(ar), -jnp.sin(ac), jnp.sin(ac)], axis=-1)
    return cos_t, sin_t


def _outproj_kernel(ua_ref, up_ref, un_ref, yna_ref, zna_ref, yss_ref, x_ref, gate_ref, gpost_ref, cw_ref,
                    w_ref, o_ref):
    j = pl.program_id(1)
    nt = pl.num_programs(1)
    tm = ua_ref.shape[0]
    cd = CONV_DIM
    rows8 = lax.broadcasted_iota(jnp.int32, (8, 1), 0)
    ua = ua_ref[...]
    t = ua[:, 2 * cd:3 * cd].astype(F32) * ua[:, 0:cd].astype(F32)
    up = up_ref[15:16, :]
    un = un_ref[0:1, :]
    tprev = up[:, 2 * cd:3 * cd].astype(F32) * up[:, 0:cd].astype(F32) * (j > 0).astype(F32)
    tnext = un[:, 2 * cd:3 * cd].astype(F32) * un[:, 0:cd].astype(F32) * (j < nt - 1).astype(F32)
    tm1 = pltpu.roll(t, 1, 0)
    tm1 = jnp.concatenate([jnp.where(rows8 == 0, tprev, tm1[0:8]), tm1[8:]], axis=0)
    tp1 = pltpu.roll(t, tm - 1, 0)
    tp1 = jnp.concatenate([tp1[:tm - 8], jnp.where(rows8 == 7, tnext, tp1[tm - 8:])], axis=0)
    conv = tm1 * cw_ref[0:1, :] + t * cw_ref[1:2, :] + tp1 * cw_ref[2:3, :]
    ysc = ua[:, cd:2 * cd].astype(F32) * conv * _silu(ua[:, 3 * cd:4 * cd].astype(F32))
    yna = yna_ref[...].astype(F32) * _silu(zna_ref[...].astype(F32))
    ycat = jnp.concatenate([ysc.astype(BF16), yna.astype(BF16), yss_ref[...]], axis=-1)
    out = jnp.dot(ycat, w_ref[...], preferred_element_type=F32)
    o_ref[...] = x_ref[...] + gate_ref[...] * (_rms(out) * gpost_ref[...])


def _outproj(u, y_na, y_ss, x, gate, g_post, conv_w, w_out, tm):
    b, l, d = x.shape
    hb = tm // 16
    n_hb = l // 16
    zb = COL_ZNA // NA_DIM
    return pl.pallas_call(
        _outproj_kernel,
        out_shape=jax.ShapeDtypeStruct((b, l, d), F32),
        grid=(b, l // tm),
        in_specs=[
            pl.BlockSpec((None, tm, 4 * CONV_DIM), lambda i, j: (i, j, 0)),
            pl.BlockSpec((None, 16, 4 * CONV_DIM), lambda i, j: (i, jnp.maximum(j * hb - 1, 0), 0)),
            pl.BlockSpec((None, 16, 4 * CONV_DIM), lambda i, j: (i, jnp.minimum((j + 1) * hb, n_hb - 1), 0)),
            pl.BlockSpec((None, tm, NA_DIM), lambda i, j: (i, j, 0)),
            pl.BlockSpec((None, tm, NA_DIM), lambda i, j: (i, j, zb)),
            pl.BlockSpec((None, tm, SSD_DIM), lambda i, j: (i, j, 0)),
            pl.BlockSpec((None, tm, d), lambda i, j: (i, j, 0)),
            pl.BlockSpec((None, 1, d), lambda i, j: (i, 0, 0)),
            pl.BlockSpec((1, d), lambda i, j: (0, 0)),
            pl.BlockSpec((3, CONV_DIM), lambda i, j: (0, 0)),
            pl.BlockSpec((d, d), lambda i, j: (0, 0)),
        ],
        out_specs=pl.BlockSpec((None, tm, d), lambda i, j: (i, j, 0)),
        compiler_params=_params("arbitrary", "arbitrary"),
        name="outproj",
    )(u, u, u, y_na, u, y_ss, x, gate, g_post, conv_w, w_out)


def kernel(x, c, ctx, c_ctx, w_ada, b_ada, g_pre, g_post, w_in, conv_a_w, rpb, ssd_conv_w, ssd_conv_b,
           dt_bias_f, dt_bias_b, a_log_f, a_log_b, ssd_d, ssd_norm_w, w_out):
    depth = w_ada.shape[0]
    bsz, seq, d = x.shape
    n_ctx = ctx.shape[1]
    assert d == D_MODEL and seq % (2 * ATTN_ROWS * GRID_W) == 0 and seq // GRID_W >= ATTN_KEY_ROWS
    assert n_ctx % SSD_CHUNK == 0 and w_in.shape[-1] == U_DIM + 2 * SSD_HEADS
    tm_lat = min(512, seq)
    tm_ctx = n_ctx

    n_rows = -(-(bsz + 1) // 8) * 8
    cc = jnp.zeros((n_rows, d), F32).at[:bsz].set(c).at[bsz].set(c_ctx)
    mods = _ada_all_layers(cc, w_ada, b_ada)
    cos_t, sin_t = _rope_tables(seq)
    pad8 = DT_PAD - 2 * SSD_HEADS

    x_ctx = ctx
    for layer in range(depth):
        m = mods[layer]
        shift, scale, gate = (m[:bsz, k * d:(k + 1) * d].reshape(bsz, 1, d) for k in range(3))
        shift_c, scale_c, gate_c = (jnp.broadcast_to(m[bsz, k * d:(k + 1) * d], (bsz, 1, d)) for k in range(3))
        w_u = w_in[layer, :, :U_DIM].astype(BF16)
        w_dt = jnp.pad(w_in[layer, :, U_DIM:], ((0, 0), (0, pad8))).astype(BF16)
        gpre = g_pre[layer].reshape(1, d)
        gpost = g_post[layer].reshape(1, d)
        per_head = lambda f, bwd: jnp.broadcast_to(jnp.concatenate([f, bwd])[:, None], (2 * SSD_HEADS, SSD_CHUNK))
        dt_bias = per_head(dt_bias_f[layer], dt_bias_b[layer])
        a_log = per_head(a_log_f[layer], a_log_b[layer])
        d_vec = jnp.repeat(ssd_d[layer], SSD_HEAD_DIM).reshape(1, SSD_DIM)
        bias = _attn_bias_table(rpb[layer])

        u_lat, dt_lat = _inproj(x, shift, scale, gpre, w_u, w_dt, tm_lat)
        u_ctx, dt_ctx = _inproj(x_ctx, shift_c, scale_c, gpre, w_u, w_dt, tm_ctx)
        y_na = _attention(u_lat, u_ctx, bias)
        y_ss, y_ss_ctx = _ssd(u_lat, dt_lat, u_ctx, dt_ctx, ssd_conv_w[layer], ssd_conv_b[layer].reshape(1, -1),
                              dt_bias, a_log, d_vec, ssd_norm_w[layer].reshape(1, -1), cos_t, sin_t)
        w_o = w_out[layer].astype(BF16)
        x = _outproj(u_lat, y_na, y_ss, x, gate, gpost, conv_a_w[layer], w_o, tm_lat)
        if layer < depth - 1:
            y_na_ctx = _ctx_attention(u_ctx)
            x_ctx = _outproj(u_ctx, y_na_ctx, y_ss_ctx, x_ctx, gate_c, gpost, conv_a_w[layer], w_o, tm_ctx)
    return x
```

```python
import functools
import math

import jax
import jax.numpy as jnp
import numpy as np
from jax import lax
from jax.experimental import pallas as pl
from jax.experimental.pallas import tpu as pltpu

F32 = jnp.float32
BF16 = jnp.bfloat16
HIGHEST = lax.Precision.HIGHEST

D_MODEL = 1024
GRID_W = 64
EPS = 1e-6
CONV_DIM = 256
NA_HEADS = 8
NA_HEAD_DIM = 64
NA_DIM = NA_HEADS * NA_HEAD_DIM
WIN_H = 8
WIN_W = 16
SSD_HEADS = 4
SSD_HEAD_DIM = 64
SSD_DIM = SSD_HEADS * SSD_HEAD_DIM
SSD_GROUPS = 2
SSD_STATE = 128
SSD_CHUNK = 128
XBC_DIM = SSD_DIM + 2 * SSD_GROUPS * SSD_STATE
ROPE_BASE = 10000.0
U_DIM = 4 * CONV_DIM + 4 * NA_DIM + XBC_DIM + SSD_DIM
DT_PAD = 128
COL_Q = 4 * CONV_DIM
COL_K = COL_Q + NA_DIM
COL_V = COL_K + NA_DIM
COL_ZNA = COL_V + NA_DIM
COL_XBC = COL_ZNA + NA_DIM
COL_ZSS = COL_XBC + XBC_DIM

LANE = 128
ATTN_ROWS = 4
ATTN_KEY_ROWS = 12
ATTN_STRIP = 16
SSD_UNROLL = 4
VMEM_LIMIT = 56 * 1024 * 1024

NEG_INF = float("-inf")


def _sigmoid(x):
    return 1.0 / (1.0 + jnp.exp(-x))


def _silu(x):
    return x * _sigmoid(x)


def _softplus(x):
    return jnp.maximum(x, 0.0) + jnp.log1p(jnp.exp(-jnp.abs(x)))


def _rms(x):
    return x * lax.rsqrt(jnp.mean(x * x, axis=-1, keepdims=True) + EPS)


def _params(*sem):
    return pltpu.CompilerParams(dimension_semantics=sem, vmem_limit_bytes=VMEM_LIMIT)


def _ada_kernel(cc_ref, w_ref, b_ref, o_ref):
    s = _silu(cc_ref[...])
    o_ref[...] = jnp.dot(s, w_ref[...], preferred_element_type=F32, precision=HIGHEST) + b_ref[...]


def _ada_all_layers(cc, w_ada, b_ada):
    depth = w_ada.shape[0]
    r = cc.shape[0]
    nblk = 3 * D_MODEL // D_MODEL
    return pl.pallas_call(
        _ada_kernel,
        out_shape=jax.ShapeDtypeStruct((depth, r, 3 * D_MODEL), F32),
        grid=(depth, nblk),
        in_specs=[
            pl.BlockSpec((r, D_MODEL), lambda l, n: (0, 0)),
            pl.BlockSpec((None, D_MODEL, D_MODEL), lambda l, n: (l, 0, n)),
            pl.BlockSpec((None, 1, D_MODEL), lambda l, n: (l, 0, n)),
        ],
        out_specs=pl.BlockSpec((None, r, D_MODEL), lambda l, n: (l, 0, n)),
        compiler_params=_params("arbitrary", "arbitrary"),
        name="ada",
    )(cc, w_ada, b_ada.reshape(depth, 1, 3 * D_MODEL))


def _inproj_kernel(x_ref, shift_ref, scale_ref, g_ref, w_ref, wdt_ref, u_ref, dt_ref, dt_scr):
    h = _rms(x_ref[...]) * g_ref[...]
    h = h * (1.0 + scale_ref[...]) + shift_ref[...]
    hb = h.astype(BF16)
    nb = 512
    for n in range(0, U_DIM, nb):
        u_ref[:, n:n + nb] = jnp.dot(hb, w_ref[:, n:n + nb], preferred_element_type=F32).astype(BF16)
    dt_scr[...] = jnp.dot(hb, wdt_ref[...], preferred_element_type=F32)
    dt_ref[...] = dt_scr[...].T[0:2 * SSD_HEADS, :]


def _inproj(x, shift, scale, g_pre, w_u, w_dt, tm):
    b, l, d = x.shape
    return pl.pallas_call(
        _inproj_kernel,
        out_shape=(jax.ShapeDtypeStruct((b, l, U_DIM), BF16), jax.ShapeDtypeStruct((b, 2 * SSD_HEADS, l), F32)),
        grid=(b, l // tm),
        in_specs=[
            pl.BlockSpec((None, tm, d), lambda i, j: (i, j, 0)),
            pl.BlockSpec((None, 1, d), lambda i, j: (i, 0, 0)),
            pl.BlockSpec((None, 1, d), lambda i, j: (i, 0, 0)),
            pl.BlockSpec((1, d), lambda i, j: (0, 0)),
            pl.BlockSpec((d, U_DIM), lambda i, j: (0, 0)),
            pl.BlockSpec((d, DT_PAD), lambda i, j: (0, 0)),
        ],
        out_specs=(
            pl.BlockSpec((None, tm, U_DIM), lambda i, j: (i, j, 0)),
            pl.BlockSpec((None, 2 * SSD_HEADS, tm), lambda i, j: (i, 0, j)),
        ),
        scratch_shapes=[pltpu.VMEM((tm, DT_PAD), F32)],
        compiler_params=_params("arbitrary", "arbitrary"),
        name="inproj",
    )(x, shift, scale, g_pre, w_u, w_dt)


def _attn_variants(n_rows):
    return {
        "top": dict(jlo=[0, 0, 0, 0], droff=0),
        "mid": dict(jlo=[0, 1, 2, 3], droff=-WIN_H // 2),
        "bot": dict(jlo=[4, 4, 4, 4], droff=-WIN_H),
    }


def _head_scales():
    lane1 = lax.broadcasted_iota(jnp.int32, (1, LANE), 1)
    scale = NA_HEAD_DIM ** -0.5
    lo = jnp.where(lane1 < NA_HEAD_DIM, scale, 0.0).astype(BF16)
    hi = jnp.where(lane1 >= NA_HEAD_DIM, scale, 0.0).astype(BF16)
    return lo, hi


def _softmax_strip(s_scr, p_scr, l_scr, bias_ref, hh, row0, blocks, n_ctx_blocks, n_win_blocks):
    rows = slice(row0, row0 + ATTN_STRIP)
    brow = (row0 % GRID_W)
    lane = lax.broadcasted_iota(jnp.int32, (ATTN_STRIP, LANE), 1)
    vals = []
    for m, d, keep in blocks:
        s = s_scr[rows, m * LANE:(m + 1) * LANE] + bias_ref[hh, d, brow:brow + ATTN_STRIP, :]
        if keep == "lo":
            s = jnp.where(lane < GRID_W, s, NEG_INF)
        elif keep == "hi":
            s = jnp.where(lane >= GRID_W, s, NEG_INF)
        vals.append((m, s))
    for c in range(n_ctx_blocks):
        m = n_win_blocks + c
        vals.append((m, s_scr[rows, m * LANE:(m + 1) * LANE]))
    mx = vals[0][1]
    for _, s in vals[1:]:
        mx = jnp.maximum(mx, s)
    mx = jnp.max(mx, axis=-1, keepdims=True)
    tot = None
    used = set()
    for m, s in vals:
        p = jnp.exp(s - mx)
        tot = p if tot is None else tot + p
        p_scr[rows, m * LANE:(m + 1) * LANE] = p.astype(BF16)
        used.add(m)
    for m in range(n_win_blocks):
        if m not in used:
            p_scr[rows, m * LANE:(m + 1) * LANE] = jnp.zeros((ATTN_STRIP, LANE), BF16)
    l_scr[rows, :] = jnp.broadcast_to(jnp.sum(tot, axis=-1, keepdims=True), (ATTN_STRIP, LANE))


def _attn_kernel(q_ref, k_ref, v_ref, kc_ref, vc_ref, bias_ref, o_ref, s0, s1, p0, p1, l0, l1, *, n_rows):
    n_units = n_rows // ATTN_ROWS
    n_q = ATTN_ROWS * GRID_W
    n_win = ATTN_KEY_ROWS * GRID_W
    n_ctx = kc_ref.shape[0]
    n_win_blocks = n_win // LANE
    n_ctx_blocks = n_ctx // LANE
    variants = _attn_variants(n_rows)
    lane_q = lax.broadcasted_iota(jnp.int32, (n_q, LANE), 1)
    head_scale = _head_scales()
    contract_last = (((1,), (1,)), ((), ()))
    s_bufs, p_bufs, l_bufs = (s0, s1), (p0, p1), (l0, l1)

    def key_start(g):
        kb = jnp.clip(g * ATTN_ROWS - WIN_H // 2, 0, n_rows - ATTN_KEY_ROWS)
        return pl.multiple_of(kb * GRID_W, GRID_W)

    def scores(g, slot):
        s_scr = s_bufs[slot]
        q = q_ref[pl.ds(pl.multiple_of(g * n_q, n_q), n_q), :]
        q2 = jnp.concatenate([q * head_scale[0], q * head_scale[1]], axis=0)
        kw = k_ref[pl.ds(key_start(g), n_win), :]
        s_scr[:, 0:n_win] = lax.dot_general(q2, kw, contract_last, preferred_element_type=F32)
        s_scr[:, n_win:n_win + n_ctx] = lax.dot_general(q2, kc_ref[...], contract_last, preferred_element_type=F32)

    def softmax(kind, slot):
        geo = variants[kind]
        for hh in range(2):
            for i in range(ATTN_ROWS):
                jlo = geo["jlo"][i]
                blocks = []
                for m in range(n_win_blocks):
                    jl, jr = 2 * m, 2 * m + 1
                    vl = jlo <= jl < jlo + WIN_H
                    vr = jlo <= jr < jlo + WIN_H
                    if not (vl or vr):
                        continue
                    d = (jl - i + geo["droff"]) + WIN_H
                    blocks.append((m, d, None if (vl and vr) else ("lo" if vl else "hi")))
                for part in range(GRID_W // ATTN_STRIP):
                    row0 = hh * n_q + i * GRID_W + part * ATTN_STRIP
                    _softmax_strip(s_bufs[slot], p_bufs[slot], l_bufs[slot], bias_ref, hh, row0, blocks,
                                   n_ctx_blocks, n_win_blocks)

    def pv(g, slot):
        p_scr = p_bufs[slot]
        vw = v_ref[pl.ds(key_start(g), n_win), :]
        o2 = jnp.dot(p_scr[:, 0:n_win], vw, preferred_element_type=F32)
        o2 = o2 + jnp.dot(p_scr[:, n_win:n_win + n_ctx], vc_ref[...], preferred_element_type=F32)
        o2 = o2 * (1.0 / l_bufs[slot][...])
        out = jnp.where(lane_q < GRID_W, o2[0:n_q], o2[n_q:2 * n_q])
        o_ref[pl.ds(pl.multiple_of(g * n_q, n_q), n_q), :] = out.astype(o_ref.dtype)

    scores(jnp.int32(0), 0)
    scores(jnp.int32(1), 1)
    softmax("top", 0)

    def pair_body(tt, carry):
        t = 1 + 2 * tt
        softmax("mid", 1)
        pv(t - 1, 0)
        scores(t + 1, 0)
        softmax("mid", 0)
        pv(t, 1)
        scores(t + 2, 1)
        return carry

    lax.fori_loop(0, (n_units - 2) // 2, pair_body, 0)
    softmax("bot", 1)
    pv(jnp.int32(n_units - 2), 0)
    pv(jnp.int32(n_units - 1), 1)


def _attention(u, u_ctx, bias):
    b, l, _ = u.shape
    n_ctx = u_ctx.shape[1]
    n_rows = l // GRID_W
    n_pairs = NA_HEADS // 2
    n_q = ATTN_ROWS * GRID_W
    n_keys = ATTN_KEY_ROWS * GRID_W + n_ctx
    qb, kb, vb = COL_Q // LANE, COL_K // LANE, COL_V // LANE
    return pl.pallas_call(
        functools.partial(_attn_kernel, n_rows=n_rows),
        out_shape=jax.ShapeDtypeStruct((b, l, NA_DIM), BF16),
        grid=(b, n_pairs),
        in_specs=[
            pl.BlockSpec((None, l, LANE), lambda i, p: (i, 0, qb + p)),
            pl.BlockSpec((None, l, LANE), lambda i, p: (i, 0, kb + p)),
            pl.BlockSpec((None, l, LANE), lambda i, p: (i, 0, vb + p)),
            pl.BlockSpec((None, n_ctx, LANE), lambda i, p: (i, 0, kb + p)),
            pl.BlockSpec((None, n_ctx, LANE), lambda i, p: (i, 0, vb + p)),
            pl.BlockSpec((2, 2 * WIN_H, GRID_W, LANE), lambda i, p: (p, 0, 0, 0)),
        ],
        out_specs=pl.BlockSpec((None, l, LANE), lambda i, p: (i, 0, p)),
        scratch_shapes=[
            pltpu.VMEM((2 * n_q, n_keys), F32),
            pltpu.VMEM((2 * n_q, n_keys), F32),
            pltpu.VMEM((2 * n_q, n_keys), BF16),
            pltpu.VMEM((2 * n_q, n_keys), BF16),
            pltpu.VMEM((2 * n_q, LANE), F32),
            pltpu.VMEM((2 * n_q, LANE), F32),
        ],
        compiler_params=_params("arbitrary", "arbitrary"),
        name="natten",
    )(u, u, u, u_ctx, u_ctx, bias)


def _ctx_attn_kernel(q_ref, k_ref, v_ref, o_ref):
    n_q = q_ref.shape[0]
    lane_q = lax.broadcasted_iota(jnp.int32, (n_q, LANE), 1)
    head_scale = _head_scales()
    q = q_ref[...]
    q2 = jnp.concatenate([q * head_scale[0], q * head_scale[1]], axis=0)
    s = lax.dot_general(q2, k_ref[...], (((1,), (1,)), ((), ())), preferred_element_type=F32)
    p = jnp.exp(s - jnp.max(s, axis=-1, keepdims=True))
    rinv = 1.0 / jnp.sum(p, axis=-1, keepdims=True)
    o2 = jnp.dot(p.astype(BF16), v_ref[...], preferred_element_type=F32) * rinv
    o_ref[...] = jnp.where(lane_q < GRID_W, o2[0:n_q], o2[n_q:2 * n_q]).astype(o_ref.dtype)


def _ctx_attention(u_ctx):
    b, n_ctx, _ = u_ctx.shape
    qb, kb, vb = COL_Q // LANE, COL_K // LANE, COL_V // LANE
    return pl.pallas_call(
        _ctx_attn_kernel,
        out_shape=jax.ShapeDtypeStruct((b, n_ctx, NA_DIM), BF16),
        grid=(b, NA_HEADS // 2),
        in_specs=[
            pl.BlockSpec((None, n_ctx, LANE), lambda i, p: (i, 0, qb + p)),
            pl.BlockSpec((None, n_ctx, LANE), lambda i, p: (i, 0, kb + p)),
            pl.BlockSpec((None, n_ctx, LANE), lambda i, p: (i, 0, vb + p)),
        ],
        out_specs=pl.BlockSpec((None, n_ctx, LANE), lambda i, p: (i, 0, p)),
        compiler_params=_params("arbitrary", "arbitrary"),
        name="ctx_attn",
    )(u_ctx, u_ctx, u_ctx)


def _attn_bias_table(rpb):
    heads = rpb.shape[0]
    cols = np.arange(GRID_W)
    c0 = np.clip(cols - WIN_W // 2, 0, GRID_W - WIN_W)
    rel = cols[None, :] - cols[:, None] + (WIN_W - 1)
    inside = (cols[None, :] >= c0[:, None]) & (cols[None, :] < c0[:, None] + WIN_W)
    select = ((rel[None] == np.arange(2 * WIN_W - 1)[:, None, None]) & inside[None]).astype(np.float32)
    blk = jnp.einsum("hrj,jck->hrck", rpb.astype(F32), jnp.asarray(select), precision=HIGHEST)
    blk = jnp.where(jnp.asarray(inside)[None, None], blk, NEG_INF)
    pad = jnp.full((heads, 1, GRID_W, GRID_W), NEG_INF, F32)
    left = jnp.concatenate([pad, blk], axis=1)
    right = jnp.concatenate([blk, pad], axis=1)
    return jnp.concatenate([left, right], axis=-1).astype(F32)


def _ssd_kernel(xl_ref, zl_ref, dtl_ref, xc_ref, zc_ref, dtc_ref, cw_ref, cb_ref, dtb_ref, alog_ref,
                dvec_ref, nw_ref, cos_ref, sin_ref, yl_ref, yc_ref,
                xs_l, bc_l, xs_c, bc_c, yacc_l, yacc_c):
    q = SSD_CHUNK
    ri = lax.broadcasted_iota(jnp.int32, (q, q), 0)
    ci = lax.broadcasted_iota(jnp.int32, (q, q), 1)
    lane = lax.broadcasted_iota(jnp.int32, (q, LANE), 1)
    lane1 = lax.broadcasted_iota(jnp.int32, (1, LANE), 1)
    rows8 = lax.broadcasted_iota(jnp.int32, (8, 1), 0)
    lo_half = lane < SSD_HEAD_DIM
    lo_half1 = lane1 < SSD_HEAD_DIM
    a_all = -jnp.exp(alog_ref[...])
    n_hd = 2 * SSD_HEADS

    def prep(raw_ref, c, n_chunks, rope, xs_ref, bc_ref):
        base = pl.multiple_of(c * q, q)
        seq = n_chunks * q
        x = raw_ref[pl.ds(base, q), :].astype(F32)
        pstart = pl.multiple_of(jnp.maximum(base - 16, 0), 16)
        nstart = pl.multiple_of(jnp.minimum(base + q, seq - 16), 16)
        prev = raw_ref[pl.ds(pstart, 16), :][15:16, :].astype(F32) * (c > 0).astype(F32)
        nxt = raw_ref[pl.ds(nstart, 16), :][0:1, :].astype(F32) * (c < n_chunks - 1).astype(F32)
        xm1 = pltpu.roll(x, 1, 0)
        xm1 = jnp.concatenate([jnp.where(rows8 == 0, prev, xm1[0:8]), xm1[8:]], axis=0)
        xp1 = pltpu.roll(x, q - 1, 0)
        xp1 = jnp.concatenate([xp1[:q - 8], jnp.where(rows8 == 7, nxt, xp1[q - 8:])], axis=0)
        y = xm1 * cw_ref[0:1, :] + x * cw_ref[1:2, :] + xp1 * cw_ref[2:3, :] + cb_ref[...]
        y = _silu(y)
        xs_ref[pl.ds(base, q), :] = y[:, 0:SSD_DIM]
        for t in range(2 * SSD_GROUPS):
            blk = y[:, SSD_DIM + t * SSD_STATE:SSD_DIM + (t + 1) * SSD_STATE]
            if rope:
                sw = jnp.where(jnp.bitwise_and(lane, 63) < 32, pltpu.roll(blk, 96, 1), pltpu.roll(blk, 32, 1))
                blk = blk * cos_ref[pl.ds(base, q), :] + sw * sin_ref[pl.ds(base, q), :]
            bc_ref[pl.ds(base, q), t * SSD_STATE:(t + 1) * SSD_STATE] = blk.astype(BF16)

    def scan_chunk(c, direction, dt_ref, xs_ref, bc_ref, states):
        base = pl.multiple_of(c * q, q)
        dt_t = _softplus(dt_ref[:, pl.ds(base, q)] + dtb_ref[...])
        tri = jnp.where(ri <= ci, 1.0, 0.0) if direction == 0 else jnp.where(ri >= ci, 1.0, 0.0)
        acs_t = jnp.dot(dt_t * a_all, tri.astype(F32), preferred_element_type=F32, precision=HIGHEST)
        stacked = jnp.concatenate([dt_t, acs_t, jnp.zeros((q - 2 * n_hd, q), F32)], axis=0)
        cols_all = stacked.T
        mask = (ri >= ci) if direction == 0 else (ri <= ci)
        end = q - 1 if direction == 0 else 0
        ys = []
        new_states = []
        for g in range(SSD_GROUPS):
            xg = xs_ref[pl.ds(base, q), g * LANE:(g + 1) * LANE]
            bg = bc_ref[pl.ds(base, q), g * SSD_STATE:(g + 1) * SSD_STATE]
            cg = bc_ref[pl.ds(base, q), (SSD_GROUPS + g) * SSD_STATE:(SSD_GROUPS + g + 1) * SSD_STATE]
            gram = lax.dot_general(cg, bg, (((1,), (1,)), ((), ())), preferred_element_type=F32)
            heads = [direction * SSD_HEADS + 2 * g, direction * SSD_HEADS + 2 * g + 1]
            a_bc = [jnp.broadcast_to(cols_all[:, n_hd + k:n_hd + k + 1], (q, LANE)) for k in heads]
            d_bc = [jnp.broadcast_to(cols_all[:, k:k + 1], (q, LANE)) for k in heads]
            a_row = [acs_t[k:k + 1, :] for k in heads]
            a_end = [acs_t[k:k + 1, end:end + 1] for k in heads]
            dtp = jnp.where(lo_half, d_bc[0], d_bc[1])
            acp = jnp.where(lo_half, a_bc[0], a_bc[1])
            a_end_p = jnp.where(lo_half1, a_end[0], a_end[1])
            xdt = (xg * dtp).astype(BF16)
            yd = []
            for k in range(2):
                decay = jnp.exp(jnp.where(mask, a_bc[k] - a_row[k], NEG_INF))
                yd.append(jnp.dot((gram * decay).astype(BF16), xdt, preferred_element_type=F32))
            y_diag = jnp.where(lo_half, yd[0], yd[1])
            st = states[g]
            y_off = jnp.dot(cg, st.astype(BF16), preferred_element_type=F32) * jnp.exp(acp)
            xw = (xg * (jnp.exp(a_end_p - acp) * dtp)).astype(BF16)
            bgt = bg.astype(F32).T.astype(BF16)
            upd = jnp.dot(bgt, xw, preferred_element_type=F32)
            new_states.append(st * jnp.exp(a_end_p) + upd)
            ys.append(y_diag + y_off)
        return jnp.concatenate(ys, axis=-1), tuple(new_states)

    def finish(y, c, xs_ref, z_ref, out_ref):
        base = pl.multiple_of(c * q, q)
        y = y + dvec_ref[...] * xs_ref[pl.ds(base, q), :]
        y = y * _silu(z_ref[pl.ds(base, q), :].astype(F32))
        out_ref[pl.ds(base, q), :] = (_rms(y) * nw_ref[...]).astype(out_ref.dtype)

    n_c = xc_ref.shape[0] // q
    n_l = xl_ref.shape[0] // q

    def fwd_pass(raw_ref, dt_ref, n_chunks, rope, xs_ref, bc_ref, yacc_ref, states):
        def body(c, states):
            prep(raw_ref, c, n_chunks, rope, xs_ref, bc_ref)
            base = pl.multiple_of(c * q, q)
            y, states = scan_chunk(c, 0, dt_ref, xs_ref, bc_ref, states)
            yacc_ref[pl.ds(base, q), :] = y
            return states

        return lax.fori_loop(0, n_chunks, body, states, unroll=min(n_chunks, SSD_UNROLL))

    def bwd_pass(dt_ref, n_chunks, xs_ref, bc_ref, yacc_ref, z_ref, out_ref, states):
        def body(t, states):
            c = n_chunks - 1 - t
            base = pl.multiple_of(c * q, q)
            y, states = scan_chunk(c, 1, dt_ref, xs_ref, bc_ref, states)
            finish(y + yacc_ref[pl.ds(base, q), :], c, xs_ref, z_ref, out_ref)
            return states

        return lax.fori_loop(0, n_chunks, body, states, unroll=min(n_chunks, SSD_UNROLL))

    zero_states = tuple(jnp.zeros((SSD_STATE, LANE), F32) for _ in range(SSD_GROUPS))
    states = fwd_pass(xc_ref, dtc_ref, n_c, False, xs_c, bc_c, yacc_c, zero_states)
    fwd_pass(xl_ref, dtl_ref, n_l, True, xs_l, bc_l, yacc_l, states)
    states = bwd_pass(dtc_ref, n_c, xs_c, bc_c, yacc_c, zc_ref, yc_ref, zero_states)
    bwd_pass(dtl_ref, n_l, xs_l, bc_l, yacc_l, zl_ref, yl_ref, states)


def _ssd(u, dt, u_ctx, dt_ctx, conv_w, conv_b, dt_bias, a_log, d_vec, norm_w, cos_t, sin_t):
    b, l, _ = u.shape
    n_ctx = u_ctx.shape[1]
    xb = COL_XBC // XBC_DIM
    zb = COL_ZSS // SSD_DIM
    n_bc = 2 * SSD_GROUPS * SSD_STATE
    const = lambda shape: pl.BlockSpec(shape, lambda i: (0,) * len(shape))
    return pl.pallas_call(
        _ssd_kernel,
        out_shape=(jax.ShapeDtypeStruct((b, l, SSD_DIM), BF16), jax.ShapeDtypeStruct((b, n_ctx, SSD_DIM), BF16)),
        grid=(b,),
        in_specs=[
            pl.BlockSpec((None, l, XBC_DIM), lambda i: (i, 0, xb)),
            pl.BlockSpec((None, l, SSD_DIM), lambda i: (i, 0, zb)),
            pl.BlockSpec((None, 2 * SSD_HEADS, l), lambda i: (i, 0, 0)),
            pl.BlockSpec((None, n_ctx, XBC_DIM), lambda i: (i, 0, xb)),
            pl.BlockSpec((None, n_ctx, SSD_DIM), lambda i: (i, 0, zb)),
            pl.BlockSpec((None, 2 * SSD_HEADS, n_ctx), lambda i: (i, 0, 0)),
            const((3, XBC_DIM)),
            const((1, XBC_DIM)),
            const((2 * SSD_HEADS, SSD_CHUNK)),
            const((2 * SSD_HEADS, SSD_CHUNK)),
            const((1, SSD_DIM)),
            const((1, SSD_DIM)),
            const((l, SSD_STATE)),
            const((l, SSD_STATE)),
        ],
        out_specs=(
            pl.BlockSpec((None, l, SSD_DIM), lambda i: (i, 0, 0)),
            pl.BlockSpec((None, n_ctx, SSD_DIM), lambda i: (i, 0, 0)),
        ),
        scratch_shapes=[
            pltpu.VMEM((l, SSD_DIM), F32),
            pltpu.VMEM((l, n_bc), BF16),
            pltpu.VMEM((n_ctx, SSD_DIM), F32),
            pltpu.VMEM((n_ctx, n_bc), BF16),
            pltpu.VMEM((l, SSD_DIM), F32),
            pltpu.VMEM((n_ctx, SSD_DIM), F32),
        ],
        compiler_params=_params("arbitrary"),
        name="ssd",
    )(u, u, dt, u_ctx, u_ctx, dt_ctx, conv_w, conv_b, dt_bias, a_log, d_vec, norm_w, cos_t, sin_t)


def _rope_tables(seq):
    n_freq = SSD_STATE // 4
    pos = jnp.arange(seq)
    row_pos = (pos // GRID_W).astype(F32)
    col_pos = (pos % GRID_W).astype(F32)
    inv_freq = ROPE_BASE ** (-jnp.arange(n_freq, dtype=F32) / n_freq)
    ar = row_pos[:, None] * inv_freq
    ac = col_pos[:, None] * inv_freq
    cos_t = jnp.concatenate([jnp.cos(ar), jnp.cos(ar), jnp.cos(ac), jnp.cos(ac)], axis=-1)
    sin_t = jnp.concatenate([-jnp.sin(ar), jnp.sin(ar), -jnp.sin(ac), jnp.sin(ac)], axis=-1)
    return cos_t, sin_t


def _outproj_kernel(ua_ref, up_ref, un_ref, yna_ref, zna_ref, yss_ref, x_ref, gate_ref, gpost_ref, cw_ref,
                    w_ref, o_ref):
    j = pl.program_id(1)
    nt = pl.num_programs(1)
    tm = ua_ref.shape[0]
    cd = CONV_DIM
    rows8 = lax.broadcasted_iota(jnp.int32, (8, 1), 0)
    ua = ua_ref[...]
    t = ua[:, 2 * cd:3 * cd].astype(F32) * ua[:, 0:cd].astype(F32)
    up = up_ref[15:16, :]
    un = un_ref[0:1, :]
    tprev = up[:, 2 * cd:3 * cd].astype(F32) * up[:, 0:cd].astype(F32) * (j > 0).astype(F32)
    tnext = un[:, 2 * cd:3 * cd].astype(F32) * un[:, 0:cd].astype(F32) * (j < nt - 1).astype(F32)
    tm1 = pltpu.roll(t, 1, 0)
    tm1 = jnp.concatenate([jnp.where(rows8 == 0, tprev, tm1[0:8]), tm1[8:]], axis=0)
    tp1 = pltpu.roll(t, tm - 1, 0)
    tp1 = jnp.concatenate([tp1[:tm - 8], jnp.where(rows8 == 7, tnext, tp1[tm - 8:])], axis=0)
    conv = tm1 * cw_ref[0:1, :] + t * cw_ref[1:2, :] + tp1 * cw_ref[2:3, :]
    ysc = ua[:, cd:2 * cd].astype(F32) * conv * _silu(ua[:, 3 * cd:4 * cd].astype(F32))
    yna = yna_ref[...].astype(F32) * _silu(zna_ref[...].astype(F32))
    ycat = jnp.concatenate([ysc.astype(BF16), yna.astype(BF16), yss_ref[...]], axis=-1)
    out = jnp.dot(ycat, w_ref[...], preferred_element_type=F32)
    o_ref[...] = x_ref[...] + gate_ref[...] * (_rms(out) * gpost_ref[...])


def _outproj(u, y_na, y_ss, x, gate, g_post, conv_w, w_out, tm):
    b, l, d = x.shape
    hb = tm // 16
    n_hb = l // 16
    zb = COL_ZNA // NA_DIM
    return pl.pallas_call(
        _outproj_kernel,
        out_shape=jax.ShapeDtypeStruct((b, l, d), F32),
        grid=(b, l // tm),
        in_specs=[
            pl.BlockSpec((None, tm, 4 * CONV_DIM), lambda i, j: (i, j, 0)),
            pl.BlockSpec((None, 16, 4 * CONV_DIM), lambda i, j: (i, jnp.maximum(j * hb - 1, 0), 0)),
            pl.BlockSpec((None, 16, 4 * CONV_DIM), lambda i, j: (i, jnp.minimum((j + 1) * hb, n_hb - 1), 0)),
            pl.BlockSpec((None, tm, NA_DIM), lambda i, j: (i, j, 0)),
            pl.BlockSpec((None, tm, NA_DIM), lambda i, j: (i, j, zb)),
            pl.BlockSpec((None, tm, SSD_DIM), lambda i, j: (i, j, 0)),
            pl.BlockSpec((None, tm, d), lambda i, j: (i, j, 0)),
            pl.BlockSpec((None, 1, d), lambda i, j: (i, 0, 0)),
            pl.BlockSpec((1, d), lambda i, j: (0, 0)),
            pl.BlockSpec((3, CONV_DIM), lambda i, j: (0, 0)),
            pl.BlockSpec((d, d), lambda i, j: (0, 0)),
        ],
        out_specs=pl.BlockSpec((None, tm, d), lambda i, j: (i, j, 0)),
        compiler_params=_params("arbitrary", "arbitrary"),
        name="outproj",
    )(u, u, u, y_na, u, y_ss, x, gate, g_post, conv_w, w_out)


def kernel(x, c, ctx, c_ctx, w_ada, b_ada, g_pre, g_post, w_in, conv_a_w, rpb, ssd_conv_w, ssd_conv_b,
           dt_bias_f, dt_bias_b, a_log_f, a_log_b, ssd_d, ssd_norm_w, w_out):
    depth = w_ada.shape[0]
    bsz, seq, d = x.shape
    n_ctx = ctx.shape[1]
    assert d == D_MODEL and seq % (2 * ATTN_ROWS * GRID_W) == 0 and seq // GRID_W >= ATTN_KEY_ROWS
    assert n_ctx % SSD_CHUNK == 0 and w_in.shape[-1] == U_DIM + 2 * SSD_HEADS
    tm_lat = min(512, seq)
    tm_ctx = n_ctx

    n_rows = -(-(bsz + 1) // 8) * 8
    cc = jnp.zeros((n_rows, d), F32).at[:bsz].set(c).at[bsz].set(c_ctx)
    mods = _ada_all_layers(cc, w_ada, b_ada)
    cos_t, sin_t = _rope_tables(seq)
    pad8 = DT_PAD - 2 * SSD_HEADS

    x_ctx = ctx
    for layer in range(depth):
        m = mods[layer]
        shift, scale, gate = (m[:bsz, k * d:(k + 1) * d].reshape(bsz, 1, d) for k in range(3))
        shift_c, scale_c, gate_c = (jnp.broadcast_to(m[bsz, k * d:(k + 1) * d], (bsz, 1, d)) for k in range(3))
        w_u = w_in[layer, :, :U_DIM].astype(BF16)
        w_dt = jnp.pad(w_in[layer, :, U_DIM:], ((0, 0), (0, pad8))).astype(BF16)
        gpre = g_pre[layer].reshape(1, d)
        gpost = g_post[layer].reshape(1, d)
        per_head = lambda f, bwd: jnp.broadcast_to(jnp.concatenate([f, bwd])[:, None], (2 * SSD_HEADS, SSD_CHUNK))
        dt_bias = per_head(dt_bias_f[layer], dt_bias_b[layer])
        a_log = per_head(a_log_f[layer], a_log_b[layer])
        d_vec = jnp.repeat(ssd_d[layer], SSD_HEAD_DIM).reshape(1, SSD_DIM)
        bias = _attn_bias_table(rpb[layer])

        u_lat, dt_lat = _inproj(x, shift, scale, gpre, w_u, w_dt, tm_lat)
        u_ctx, dt_ctx = _inproj(x_ctx, shift_c, scale_c, gpre, w_u, w_dt, tm_ctx)
        y_na = _attention(u_lat, u_ctx, bias)
        y_ss, y_ss_ctx = _ssd(u_lat, dt_lat, u_ctx, dt_ctx, ssd_conv_w[layer], ssd_conv_b[layer].reshape(1, -1),
                              dt_bias, a_log, d_vec, ssd_norm_w[layer].reshape(1, -1), cos_t, sin_t)
        w_o = w_out[layer].astype(BF16)
        x = _outproj(u_lat, y_na, y_ss, x, gate, gpost, conv_a_w[layer], w_o, tm_lat)
        if layer < depth - 1:
            y_na_ctx = _ctx_attention(u_ctx)
            x_ctx = _outproj(u_ctx, y_na_ctx, y_ss_ctx, x_ctx, gate_c, gpost, conv_a_w[layer], w_o, tm_ctx)
    return x
```

```python
import functools
import math

import jax
import jax.numpy as jnp
import numpy as np
from jax import lax
from jax.experimental import pallas as pl
from jax.experimental.pallas import tpu as pltpu

F32 = jnp.float32
BF16 = jnp.bfloat16
HIGHEST = lax.Precision.HIGHEST

D_MODEL = 1024
GRID_W = 64
EPS = 1e-6
CONV_DIM = 256
NA_HEADS = 8
NA_HEAD_DIM = 64
NA_DIM = NA_HEADS * NA_HEAD_DIM
WIN_H = 8
WIN_W = 16
SSD_HEADS = 4
SSD_HEAD_DIM = 64
SSD_DIM = SSD_HEADS * SSD_HEAD_DIM
SSD_GROUPS = 2
SSD_STATE = 128
SSD_CHUNK = 128
XBC_DIM = SSD_DIM + 2 * SSD_GROUPS * SSD_STATE
ROPE_BASE = 10000.0
U_DIM = 4 * CONV_DIM + 4 * NA_DIM + XBC_DIM + SSD_DIM
DT_PAD = 128
COL_Q = 4 * CONV_DIM
COL_K = COL_Q + NA_DIM
COL_V = COL_K + NA_DIM
COL_ZNA = COL_V + NA_DIM
COL_XBC = COL_ZNA + NA_DIM
COL_ZSS = COL_XBC + XBC_DIM

LANE = 128
ATTN_ROWS = 4
ATTN_KEY_ROWS = 12
ATTN_STRIP = 16
SSD_UNROLL = 4
VMEM_LIMIT = 56 * 1024 * 1024

NEG_INF = float("-inf")


def _sigmoid(x):
    return 1.0 / (1.0 + jnp.exp(-x))


def _silu(x):
    return x * _sigmoid(x)


def _softplus(x):
    return jnp.maximum(x, 0.0) + jnp.log1p(jnp.exp(-jnp.abs(x)))


def _rms(x):
    return x * lax.rsqrt(jnp.mean(x * x, axis=-1, keepdims=True) + EPS)


def _params(*sem):
    return pltpu.CompilerParams(dimension_semantics=sem, vmem_limit_bytes=VMEM_LIMIT)


def _ada_kernel(cc_ref, w_ref, b_ref, o_ref):
    s = _silu(cc_ref[...])
    o_ref[...] = jnp.dot(s, w_ref[...], preferred_element_type=F32, precision=HIGHEST) + b_ref[...]


def _ada_all_layers(cc, w_ada, b_ada):
    depth = w_ada.shape[0]
    r = cc.shape[0]
    nblk = 3 * D_MODEL // D_MODEL
    return pl.pallas_call(
        _ada_kernel,
        out_shape=jax.ShapeDtypeStruct((depth, r, 3 * D_MODEL), F32),
        grid=(depth, nblk),
        in_specs=[
            pl.BlockSpec((r, D_MODEL), lambda l, n: (0, 0)),
            pl.BlockSpec((None, D_MODEL, D_MODEL), lambda l, n: (l, 0, n)),
            pl.BlockSpec((None, 1, D_MODEL), lambda l, n: (l, 0, n)),
        ],
        out_specs=pl.BlockSpec((None, r, D_MODEL), lambda l, n: (l, 0, n)),
        compiler_params=_params("arbitrary", "arbitrary"),
        name="ada",
    )(cc, w_ada, b_ada.reshape(depth, 1, 3 * D_MODEL))


def _inproj_tile(x, shift_ref, scale_ref, g_ref, w_ref, wdt_ref, u_ref, dt_ref, dt_scr):
    h = _rms(x) * g_ref[...]
    h = h * (1.0 + scale_ref[...]) + shift_ref[...]
    hb = h.astype(BF16)
    nb = 512
    for n in range(0, U_DIM, nb):
        u_ref[:, n:n + nb] = jnp.dot(hb, w_ref[:, n:n + nb], preferred_element_type=F32).astype(BF16)
    dt_scr[...] = jnp.dot(hb, wdt_ref[...], preferred_element_type=F32)
    dt_ref[...] = dt_scr[...].T[0:2 * SSD_HEADS, :]


def _inproj_kernel(x_ref, shift_ref, scale_ref, g_ref, w_ref, wdt_ref, u_ref, dt_ref, dt_scr):
    _inproj_tile(x_ref[...], shift_ref, scale_ref, g_ref, w_ref, wdt_ref, u_ref, dt_ref, dt_scr)


def _inproj(x, shift, scale, g_pre, w_u, w_dt, tm):
    b, l, d = x.shape
    return pl.pallas_call(
        _inproj_kernel,
        out_shape=(jax.ShapeDtypeStruct((b, l, U_DIM), BF16), jax.ShapeDtypeStruct((b, 2 * SSD_HEADS, l), F32)),
        grid=(b, l // tm),
        in_specs=[
            pl.BlockSpec((None, tm, d), lambda i, j: (i, j, 0)),
            pl.BlockSpec((None, 1, d), lambda i, j: (i, 0, 0)),
            pl.BlockSpec((None, 1, d), lambda i, j: (i, 0, 0)),
            pl.BlockSpec((1, d), lambda i, j: (0, 0)),
            pl.BlockSpec((d, U_DIM), lambda i, j: (0, 0)),
            pl.BlockSpec((d, DT_PAD), lambda i, j: (0, 0)),
        ],
        out_specs=(
            pl.BlockSpec((None, tm, U_DIM), lambda i, j: (i, j, 0)),
            pl.BlockSpec((None, 2 * SSD_HEADS, tm), lambda i, j: (i, 0, j)),
        ),
        scratch_shapes=[pltpu.VMEM((tm, DT_PAD), F32)],
        compiler_params=_params("arbitrary", "arbitrary"),
        name="inproj",
    )(x, shift, scale, g_pre, w_u, w_dt)


def _attn_variants(n_rows):
    return {
        "top": dict(jlo=[0, 0, 0, 0], droff=0),
        "mid": dict(jlo=[0, 1, 2, 3], droff=-WIN_H // 2),
        "bot": dict(jlo=[4, 4, 4, 4], droff=-WIN_H),
    }


def _head_scales():
    lane1 = lax.broadcasted_iota(jnp.int32, (1, LANE), 1)
    scale = NA_HEAD_DIM ** -0.5
    lo = jnp.where(lane1 < NA_HEAD_DIM, scale, 0.0).astype(BF16)
    hi = jnp.where(lane1 >= NA_HEAD_DIM, scale, 0.0).astype(BF16)
    return lo, hi


def _softmax_strip(s_scr, p_scr, l_scr, bias_ref, hh, row0, blocks, n_ctx_blocks, n_win_blocks):
    rows = slice(row0, row0 + ATTN_STRIP)
    brow = (row0 % GRID_W)
    lane = lax.broadcasted_iota(jnp.int32, (ATTN_STRIP, LANE), 1)
    vals = []
    for m, d, keep in blocks:
        s = s_scr[rows, m * LANE:(m + 1) * LANE] + bias_ref[hh, d, brow:brow + ATTN_STRIP, :]
        if keep == "lo":
            s = jnp.where(lane < GRID_W, s, NEG_INF)
        elif keep == "hi":
            s = jnp.where(lane >= GRID_W, s, NEG_INF)
        vals.append((m, s))
    for c in range(n_ctx_blocks):
        m = n_win_blocks + c
        vals.append((m, s_scr[rows, m * LANE:(m + 1) * LANE]))
    mx = vals[0][1]
    for _, s in vals[1:]:
        mx = jnp.maximum(mx, s)
    mx = jnp.max(mx, axis=-1, keepdims=True)
    tot = None
    used = set()
    for m, s in vals:
        p = jnp.exp(s - mx)
        tot = p if tot is None else tot + p
        p_scr[rows, m * LANE:(m + 1) * LANE] = p.astype(BF16)
        used.add(m)
    for m in range(n_win_blocks):
        if m not in used:
            p_scr[rows, m * LANE:(m + 1) * LANE] = jnp.zeros((ATTN_STRIP, LANE), BF16)
    l_scr[rows, :] = jnp.broadcast_to(jnp.sum(tot, axis=-1, keepdims=True), (ATTN_STRIP, LANE))


def _attn_kernel(q_ref, k_ref, v_ref, kc_ref, vc_ref, bias_ref, o_ref, s0, s1, p0, p1, l0, l1, *, n_rows):
    n_units = n_rows // ATTN_ROWS
    n_q = ATTN_ROWS * GRID_W
    n_win = ATTN_KEY_ROWS * GRID_W
    n_ctx = kc_ref.shape[0]
    n_win_blocks = n_win // LANE
    n_ctx_blocks = n_ctx // LANE
    variants = _attn_variants(n_rows)
    lane_q = lax.broadcasted_iota(jnp.int32, (n_q, LANE), 1)
    head_scale = _head_scales()
    contract_last = (((1,), (1,)), ((), ()))
    s_bufs, p_bufs, l_bufs = (s0, s1), (p0, p1), (l0, l1)

    def key_start(g):
        kb = jnp.clip(g * ATTN_ROWS - WIN_H // 2, 0, n_rows - ATTN_KEY_ROWS)
        return pl.multiple_of(kb * GRID_W, GRID_W)

    def scores(g, slot):
        s_scr = s_bufs[slot]
        q = q_ref[pl.ds(pl.multiple_of(g * n_q, n_q), n_q), :]
        q2 = jnp.concatenate([q * head_scale[0], q * head_scale[1]], axis=0)
        kw = k_ref[pl.ds(key_start(g), n_win), :]
        s_scr[:, 0:n_win] = lax.dot_general(q2, kw, contract_last, preferred_element_type=F32)
        s_scr[:, n_win:n_win + n_ctx] = lax.dot_general(q2, kc_ref[...], contract_last, preferred_element_type=F32)

    def softmax(kind, slot):
        geo = variants[kind]
        for hh in range(2):
            for i in range(ATTN_ROWS):
                jlo = geo["jlo"][i]
                blocks = []
                for m in range(n_win_blocks):
                    jl, jr = 2 * m, 2 * m + 1
                    vl = jlo <= jl < jlo + WIN_H
                    vr = jlo <= jr < jlo + WIN_H
                    if not (vl or vr):
                        continue
                    d = (jl - i + geo["droff"]) + WIN_H
                    blocks.append((m, d, None if (vl and vr) else ("lo" if vl else "hi")))
                for part in range(GRID_W // ATTN_STRIP):
                    row0 = hh * n_q + i * GRID_W + part * ATTN_STRIP
                    _softmax_strip(s_bufs[slot], p_bufs[slot], l_bufs[slot], bias_ref, hh, row0, blocks,
                                   n_ctx_blocks, n_win_blocks)

    def pv(g, slot):
        p_scr = p_bufs[slot]
        vw = v_ref[pl.ds(key_start(g), n_win), :]
        o2 = jnp.dot(p_scr[:, 0:n_win], vw, preferred_element_type=F32)
        o2 = o2 + jnp.dot(p_scr[:, n_win:n_win + n_ctx], vc_ref[...], preferred_element_type=F32)
        o2 = o2 * (1.0 / l_bufs[slot][...])
        out = jnp.where(lane_q < GRID_W, o2[0:n_q], o2[n_q:2 * n_q])
        o_ref[pl.ds(pl.multiple_of(g * n_q, n_q), n_q), :] = out.astype(o_ref.dtype)

    scores(jnp.int32(0), 0)
    scores(jnp.int32(1), 1)
    softmax("top", 0)

    def pair_body(tt, carry):
        t = 1 + 2 * tt
        softmax("mid", 1)
        pv(t - 1, 0)
        scores(t + 1, 0)
        softmax("mid", 0)
        pv(t, 1)
        scores(t + 2, 1)
        return carry

    lax.fori_loop(0, (n_units - 2) // 2, pair_body, 0)
    softmax("bot", 1)
    pv(jnp.int32(n_units - 2), 0)
    pv(jnp.int32(n_units - 1), 1)


def _attention(u, u_ctx, bias):
    b, l, _ = u.shape
    n_ctx = u_ctx.shape[1]
    n_rows = l // GRID_W
    n_pairs = NA_HEADS // 2
    n_q = ATTN_ROWS * GRID_W
    n_keys = ATTN_KEY_ROWS * GRID_W + n_ctx
    qb, kb, vb = COL_Q // LANE, COL_K // LANE, COL_V // LANE
    return pl.pallas_call(
        functools.partial(_attn_kernel, n_rows=n_rows),
        out_shape=jax.ShapeDtypeStruct((b, l, NA_DIM), BF16),
        grid=(b, n_pairs),
        in_specs=[
            pl.BlockSpec((None, l, LANE), lambda i, p: (i, 0, qb + p)),
            pl.BlockSpec((None, l, LANE), lambda i, p: (i, 0, kb + p)),
            pl.BlockSpec((None, l, LANE), lambda i, p: (i, 0, vb + p)),
            pl.BlockSpec((None, n_ctx, LANE), lambda i, p: (i, 0, kb + p)),
            pl.BlockSpec((None, n_ctx, LANE), lambda i, p: (i, 0, vb + p)),
            pl.BlockSpec((2, 2 * WIN_H, GRID_W, LANE), lambda i, p: (p, 0, 0, 0)),
        ],
        out_specs=pl.BlockSpec((None, l, LANE), lambda i, p: (i, 0, p)),
        scratch_shapes=[
            pltpu.VMEM((2 * n_q, n_keys), F32),
            pltpu.VMEM((2 * n_q, n_keys), F32),
            pltpu.VMEM((2 * n_q, n_keys), BF16),
            pltpu.VMEM((2 * n_q, n_keys), BF16),
            pltpu.VMEM((2 * n_q, LANE), F32),
            pltpu.VMEM((2 * n_q, LANE), F32),
        ],
        compiler_params=_params("arbitrary", "arbitrary"),
        name="natten",
    )(u, u, u, u_ctx, u_ctx, bias)


def _ctx_attn_kernel(q_ref, k_ref, v_ref, o_ref):
    n_q = q_ref.shape[0]
    lane_q = lax.broadcasted_iota(jnp.int32, (n_q, LANE), 1)
    head_scale = _head_scales()
    q = q_ref[...]
    q2 = jnp.concatenate([q * head_scale[0], q * head_scale[1]], axis=0)
    s = lax.dot_general(q2, k_ref[...], (((1,), (1,)), ((), ())), preferred_element_type=F32)
    p = jnp.exp(s - jnp.max(s, axis=-1, keepdims=True))
    rinv = 1.0 / jnp.sum(p, axis=-1, keepdims=True)
    o2 = jnp.dot(p.astype(BF16), v_ref[...], preferred_element_type=F32) * rinv
    o_ref[...] = jnp.where(lane_q < GRID_W, o2[0:n_q], o2[n_q:2 * n_q]).astype(o_ref.dtype)


def _ctx_attention(u_ctx):
    b, n_ctx, _ = u_ctx.shape
    qb, kb, vb = COL_Q // LANE, COL_K // LANE, COL_V // LANE
    return pl.pallas_call(
        _ctx_attn_kernel,
        out_shape=jax.ShapeDtypeStruct((b, n_ctx, NA_DIM), BF16),
        grid=(b, NA_HEADS // 2),
        in_specs=[
            pl.BlockSpec((None, n_ctx, LANE), lambda i, p: (i, 0, qb + p)),
            pl.BlockSpec((None, n_ctx, LANE), lambda i, p: (i, 0, kb + p)),
            pl.BlockSpec((None, n_ctx, LANE), lambda i, p: (i, 0, vb + p)),
        ],
        out_specs=pl.BlockSpec((None, n_ctx, LANE), lambda i, p: (i, 0, p)),
        compiler_params=_params("arbitrary", "arbitrary"),
        name="ctx_attn",
    )(u_ctx, u_ctx, u_ctx)


def _attn_bias_table(rpb):
    heads = rpb.shape[0]
    cols = np.arange(GRID_W)
    c0 = np.clip(cols - WIN_W // 2, 0, GRID_W - WIN_W)
    rel = cols[None, :] - cols[:, None] + (WIN_W - 1)
    inside = (cols[None, :] >= c0[:, None]) & (cols[None, :] < c0[:, None] + WIN_W)
    select = ((rel[None] == np.arange(2 * WIN_W - 1)[:, None, None]) & inside[None]).astype(np.float32)
    blk = jnp.einsum("hrj,jck->hrck", rpb.astype(F32), jnp.asarray(select), precision=HIGHEST)
    blk = jnp.where(jnp.asarray(inside)[None, None], blk, NEG_INF)
    pad = jnp.full((heads, 1, GRID_W, GRID_W), NEG_INF, F32)
    left = jnp.concatenate([pad, blk], axis=1)
    right = jnp.concatenate([blk, pad], axis=1)
    return jnp.concatenate([left, right], axis=-1).astype(F32)


def _ssd_kernel(xl_ref, zl_ref, dtl_ref, xc_ref, zc_ref, dtc_ref, cw_ref, cb_ref, dtb_ref, alog_ref,
                dvec_ref, nw_ref, cos_ref, sin_ref, yl_ref, yc_ref,
                xs_l, bc_l, xs_c, bc_c, yacc_l, yacc_c):
    q = SSD_CHUNK
    ri = lax.broadcasted_iota(jnp.int32, (q, q), 0)
    ci = lax.broadcasted_iota(jnp.int32, (q, q), 1)
    lane = lax.broadcasted_iota(jnp.int32, (q, LANE), 1)
    lane1 = lax.broadcasted_iota(jnp.int32, (1, LANE), 1)
    rows8 = lax.broadcasted_iota(jnp.int32, (8, 1), 0)
    lo_half = lane < SSD_HEAD_DIM
    lo_half1 = lane1 < SSD_HEAD_DIM
    a_all = -jnp.exp(alog_ref[...])
    n_hd = 2 * SSD_HEADS

    def prep(raw_ref, c, n_chunks, rope, xs_ref, bc_ref):
        base = pl.multiple_of(c * q, q)
        seq = n_chunks * q
        x = raw_ref[pl.ds(base, q), :].astype(F32)
        pstart = pl.multiple_of(jnp.maximum(base - 16, 0), 16)
        nstart = pl.multiple_of(jnp.minimum(base + q, seq - 16), 16)
        prev = raw_ref[pl.ds(pstart, 16), :][15:16, :].astype(F32) * jnp.where(c > 0, 1.0, 0.0)
        nxt = raw_ref[pl.ds(nstart, 16), :][0:1, :].astype(F32) * jnp.where(c < n_chunks - 1, 1.0, 0.0)
        xm1 = pltpu.roll(x, 1, 0)
        xm1 = jnp.concatenate([jnp.where(rows8 == 0, prev, xm1[0:8]), xm1[8:]], axis=0)
        xp1 = pltpu.roll(x, q - 1, 0)
        xp1 = jnp.concatenate([xp1[:q - 8], jnp.where(rows8 == 7, nxt, xp1[q - 8:])], axis=0)
        y = xm1 * cw_ref[0:1, :] + x * cw_ref[1:2, :] + xp1 * cw_ref[2:3, :] + cb_ref[...]
        y = _silu(y)
        xs_ref[pl.ds(base, q), :] = y[:, 0:SSD_DIM]
        for t in range(2 * SSD_GROUPS):
            blk = y[:, SSD_DIM + t * SSD_STATE:SSD_DIM + (t + 1) * SSD_STATE]
            if rope:
                sw = jnp.where(jnp.bitwise_and(lane, 63) < 32, pltpu.roll(blk, 96, 1), pltpu.roll(blk, 32, 1))
                blk = blk * cos_ref[pl.ds(base, q), :] + sw * sin_ref[pl.ds(base, q), :]
            bc_ref[pl.ds(base, q), t * SSD_STATE:(t + 1) * SSD_STATE] = blk.astype(BF16)

    def scan_chunk(c, direction, dt_ref, xs_ref, bc_ref, states):
        base = pl.multiple_of(c * q, q)
        dt_t = _softplus(dt_ref[:, pl.ds(base, q)] + dtb_ref[...])
        tri = jnp.where(ri <= ci, 1.0, 0.0) if direction == 0 else jnp.where(ri >= ci, 1.0, 0.0)
        acs_t = jnp.dot(dt_t * a_all, tri.astype(F32), preferred_element_type=F32, precision=HIGHEST)
        stacked = jnp.concatenate([dt_t, acs_t, jnp.zeros((q - 2 * n_hd, q), F32)], axis=0)
        cols_all = stacked.T
        mask = (ri >= ci) if direction == 0 else (ri <= ci)
        end = q - 1 if direction == 0 else 0
        ys = []
        new_states = []
        for g in range(SSD_GROUPS):
            xg = xs_ref[pl.ds(base, q), g * LANE:(g + 1) * LANE]
            bg = bc_ref[pl.ds(base, q), g * SSD_STATE:(g + 1) * SSD_STATE]
            cg = bc_ref[pl.ds(base, q), (SSD_GROUPS + g) * SSD_STATE:(SSD_GROUPS + g + 1) * SSD_STATE]
            gram = lax.dot_general(cg, bg, (((1,), (1,)), ((), ())), preferred_element_type=F32)
            heads = [direction * SSD_HEADS + 2 * g, direction * SSD_HEADS + 2 * g + 1]
            a_bc = [jnp.broadcast_to(cols_all[:, n_hd + k:n_hd + k + 1], (q, LANE)) for k in heads]
            d_bc = [jnp.broadcast_to(cols_all[:, k:k + 1], (q, LANE)) for k in heads]
            a_row = [acs_t[k:k + 1, :] for k in heads]
            a_end = [acs_t[k:k + 1, end:end + 1] for k in heads]
            dtp = jnp.where(lo_half, d_bc[0], d_bc[1])
            acp = jnp.where(lo_half, a_bc[0], a_bc[1])
            a_end_p = jnp.where(lo_half1, a_end[0], a_end[1])
            xdt = (xg * dtp).astype(BF16)
            yd = []
            for k in range(2):
                decay = jnp.exp(jnp.where(mask, a_bc[k] - a_row[k], NEG_INF))
                yd.append(jnp.dot((gram * decay).astype(BF16), xdt, preferred_element_type=F32))
            y_diag = jnp.where(lo_half, yd[0], yd[1])
            st = states[g]
            y_off = jnp.dot(cg, st.astype(BF16), preferred_element_type=F32) * jnp.exp(acp)
            xw = (xg * (jnp.exp(a_end_p - acp) * dtp)).astype(BF16)
            bgt = bg.astype(F32).T.astype(BF16)
            upd = jnp.dot(bgt, xw, preferred_element_type=F32)
            new_states.append(st * jnp.exp(a_end_p) + upd)
            ys.append(y_diag + y_off)
        return jnp.concatenate(ys, axis=-1), tuple(new_states)

    def finish(y, c, xs_ref, z_ref, out_ref):
        base = pl.multiple_of(c * q, q)
        y = y + dvec_ref[...] * xs_ref[pl.ds(base, q), :]
        y = y * _silu(z_ref[pl.ds(base, q), :].astype(F32))
        out_ref[pl.ds(base, q), :] = (_rms(y) * nw_ref[...]).astype(out_ref.dtype)

    n_c = xc_ref.shape[0] // q
    n_l = xl_ref.shape[0] // q

    def fwd_pass(raw_ref, dt_ref, n_chunks, rope, xs_ref, bc_ref, yacc_ref, states):
        def body(c, states):
            prep(raw_ref, c, n_chunks, rope, xs_ref, bc_ref)
            base = pl.multiple_of(c * q, q)
            y, states = scan_chunk(c, 0, dt_ref, xs_ref, bc_ref, states)
            yacc_ref[pl.ds(base, q), :] = y
            return states

        return lax.fori_loop(0, n_chunks, body, states, unroll=min(n_chunks, SSD_UNROLL))

    def bwd_pass(dt_ref, n_chunks, xs_ref, bc_ref, yacc_ref, z_ref, out_ref, states):
        def body(t, states):
            c = n_chunks - 1 - t
            base = pl.multiple_of(c * q, q)
            y, states = scan_chunk(c, 1, dt_ref, xs_ref, bc_ref, states)
            finish(y + yacc_ref[pl.ds(base, q), :], c, xs_ref, z_ref, out_ref)
            return states

        return lax.fori_loop(0, n_chunks, body, states, unroll=min(n_chunks, SSD_UNROLL))

    zero_states = tuple(jnp.zeros((SSD_STATE, LANE), F32) for _ in range(SSD_GROUPS))
    states = fwd_pass(xc_ref, dtc_ref, n_c, False, xs_c, bc_c, yacc_c, zero_states)
    fwd_pass(xl_ref, dtl_ref, n_l, True, xs_l, bc_l, yacc_l, states)
    states = bwd_pass(dtc_ref, n_c, xs_c, bc_c, yacc_c, zc_ref, yc_ref, zero_states)
    bwd_pass(dtl_ref, n_l, xs_l, bc_l, yacc_l, zl_ref, yl_ref, states)


def _ssd(u, dt, u_ctx, dt_ctx, conv_w, conv_b, dt_bias, a_log, d_vec, norm_w, cos_t, sin_t):
    b, l, _ = u.shape
    n_ctx = u_ctx.shape[1]
    xb = COL_XBC // XBC_DIM
    zb = COL_ZSS // SSD_DIM
    n_bc = 2 * SSD_GROUPS * SSD_STATE
    const = lambda shape: pl.BlockSpec(shape, lambda i: (0,) * len(shape))
    return pl.pallas_call(
        _ssd_kernel,
        out_shape=(jax.ShapeDtypeStruct((b, l, SSD_DIM), BF16), jax.ShapeDtypeStruct((b, n_ctx, SSD_DIM), BF16)),
        grid=(b,),
        in_specs=[
            pl.BlockSpec((None, l, XBC_DIM), lambda i: (i, 0, xb)),
            pl.BlockSpec((None, l, SSD_DIM), lambda i: (i, 0, zb)),
            pl.BlockSpec((None, 2 * SSD_HEADS, l), lambda i: (i, 0, 0)),
            pl.BlockSpec((None, n_ctx, XBC_DIM), lambda i: (i, 0, xb)),
            pl.BlockSpec((None, n_ctx, SSD_DIM), lambda i: (i, 0, zb)),
            pl.BlockSpec((None, 2 * SSD_HEADS, n_ctx), lambda i: (i, 0, 0)),
            const((3, XBC_DIM)),
            const((1, XBC_DIM)),
            const((2 * SSD_HEADS, SSD_CHUNK)),
            const((2 * SSD_HEADS, SSD_CHUNK)),
            const((1, SSD_DIM)),
            const((1, SSD_DIM)),
            const((l, SSD_STATE)),
            const((l, SSD_STATE)),
        ],
        out_specs=(
            pl.BlockSpec((None, l, SSD_DIM), lambda i: (i, 0, 0)),
            pl.BlockSpec((None, n_ctx, SSD_DIM), lambda i: (i, 0, 0)),
        ),
        scratch_shapes=[
            pltpu.VMEM((l, SSD_DIM), F32),
            pltpu.VMEM((l, n_bc), BF16),
            pltpu.VMEM((n_ctx, SSD_DIM), F32),
            pltpu.VMEM((n_ctx, n_bc), BF16),
            pltpu.VMEM((l, SSD_DIM), F32),
            pltpu.VMEM((n_ctx, SSD_DIM), F32),
        ],
        compiler_params=_params("arbitrary"),
        name="ssd",
    )(u, u, dt, u_ctx, u_ctx, dt_ctx, conv_w, conv_b, dt_bias, a_log, d_vec, norm_w, cos_t, sin_t)


def _rope_tables(seq):
    n_freq = SSD_STATE // 4
    pos = jnp.arange(seq)
    row_pos = (pos // GRID_W).astype(F32)
    col_pos = (pos % GRID_W).astype(F32)
    inv_freq = ROPE_BASE ** (-jnp.arange(n_freq, dtype=F32) / n_freq)
    ar = row_pos[:, None] * inv_freq
    ac = col_pos[:, None] * inv_freq
    cos_t = jnp.concatenate([jnp.cos(ar), jnp.cos(ar), jnp.cos(ac), jnp.cos(ac)], axis=-1)
    sin_t = jnp.concatenate([-jnp.sin(ar), jnp.sin(ar), -jnp.sin(ac), jnp.sin(ac)], axis=-1)
    return cos_t, sin_t


def _outproj_tile(j, nt, ua_ref, up_ref, un_ref, yna_ref, zna_ref, yss_ref, x_ref, gate_ref, gpost_ref, cw_ref,
                  w_ref):
    tm = ua_ref.shape[0]
    cd = CONV_DIM
    rows8 = lax.broadcasted_iota(jnp.int32, (8, 1), 0)
    ua = ua_ref[...]
    t = ua[:, 2 * cd:3 * cd].astype(F32) * ua[:, 0:cd].astype(F32)
    up = up_ref[15:16, :]
    un = un_ref[0:1, :]
    tprev = up[:, 2 * cd:3 * cd].astype(F32) * up[:, 0:cd].astype(F32) * (j > 0).astype(F32)
    tnext = un[:, 2 * cd:3 * cd].astype(F32) * un[:, 0:cd].astype(F32) * (j < nt - 1).astype(F32)
    tm1 = pltpu.roll(t, 1, 0)
    tm1 = jnp.concatenate([jnp.where(rows8 == 0, tprev, tm1[0:8]), tm1[8:]], axis=0)
    tp1 = pltpu.roll(t, tm - 1, 0)
    tp1 = jnp.concatenate([tp1[:tm - 8], jnp.where(rows8 == 7, tnext, tp1[tm - 8:])], axis=0)
    conv = tm1 * cw_ref[0:1, :] + t * cw_ref[1:2, :] + tp1 * cw_ref[2:3, :]
    ysc = ua[:, cd:2 * cd].astype(F32) * conv * _silu(ua[:, 3 * cd:4 * cd].astype(F32))
    yna = yna_ref[...].astype(F32) * _silu(zna_ref[...].astype(F32))
    ycat = jnp.concatenate([ysc.astype(BF16), yna.astype(BF16), yss_ref[...]], axis=-1)
    out = jnp.dot(ycat, w_ref[...], preferred_element_type=F32)
    return x_ref[...] + gate_ref[...] * (_rms(out) * gpost_ref[...])


def _outproj_kernel(*refs):
    refs[-1][...] = _outproj_tile(pl.program_id(1), pl.num_programs(1), *refs[:-1])


N_OUTPROJ_IN = 11


def _layer_boundary_kernel(*refs):
    out_in = refs[:N_OUTPROJ_IN]
    shift_ref, scale_ref, g_ref, w_ref, wdt_ref = refs[N_OUTPROJ_IN:N_OUTPROJ_IN + 5]
    x_out_ref, u_ref, dt_ref, dt_scr, x_keep = refs[N_OUTPROJ_IN + 5:]
    j = pl.program_id(1)
    nt = pl.num_programs(1) - 1

    def in_half():
        _inproj_tile(x_keep[...], shift_ref, scale_ref, g_ref, w_ref, wdt_ref, u_ref, dt_ref, dt_scr)

    def out_half():
        x_new = _outproj_tile(j, nt, *out_in)
        x_out_ref[...] = x_new
        x_keep[...] = x_new

    @pl.when(j == 0)
    def _():
        out_half()

    @pl.when(jnp.logical_and(j > 0, j < nt))
    def _():
        in_half()
        out_half()

    @pl.when(j == nt)
    def _():
        in_half()


def _outproj_specs(l, d, tm):
    hb = tm // 16
    n_hb = l // 16
    zb = COL_ZNA // NA_DIM
    last = l // tm - 1
    tile = lambda j: jnp.minimum(j, last)
    return [
        pl.BlockSpec((None, tm, 4 * CONV_DIM), lambda i, j: (i, tile(j), 0)),
        pl.BlockSpec((None, 16, 4 * CONV_DIM), lambda i, j: (i, jnp.maximum(tile(j) * hb - 1, 0), 0)),
        pl.BlockSpec((None, 16, 4 * CONV_DIM), lambda i, j: (i, jnp.minimum((tile(j) + 1) * hb, n_hb - 1), 0)),
        pl.BlockSpec((None, tm, NA_DIM), lambda i, j: (i, tile(j), 0)),
        pl.BlockSpec((None, tm, NA_DIM), lambda i, j: (i, tile(j), zb)),
        pl.BlockSpec((None, tm, SSD_DIM), lambda i, j: (i, tile(j), 0)),
        pl.BlockSpec((None, tm, d), lambda i, j: (i, tile(j), 0)),
        pl.BlockSpec((None, 1, d), lambda i, j: (i, 0, 0)),
        pl.BlockSpec((1, d), lambda i, j: (0, 0)),
        pl.BlockSpec((3, CONV_DIM), lambda i, j: (0, 0)),
        pl.BlockSpec((d, d), lambda i, j: (0, 0)),
    ]


def _outproj(u, y_na, y_ss, x, gate, g_post, conv_w, w_out, tm):
    b, l, d = x.shape
    return pl.pallas_call(
        _outproj_kernel,
        out_shape=jax.ShapeDtypeStruct((b, l, d), F32),
        grid=(b, l // tm),
        in_specs=_outproj_specs(l, d, tm),
        out_specs=pl.BlockSpec((None, tm, d), lambda i, j: (i, j, 0)),
        compiler_params=_params("arbitrary", "arbitrary"),
        name="outproj",
    )(u, u, u, y_na, u, y_ss, x, gate, g_post, conv_w, w_out)


def _layer_boundary(u, y_na, y_ss, x, gate, g_post, conv_w, w_out, shift, scale, g_pre, w_u, w_dt, tm):
    b, l, d = x.shape
    last = l // tm - 1
    return pl.pallas_call(
        _layer_boundary_kernel,
        out_shape=(jax.ShapeDtypeStruct((b, l, d), F32), jax.ShapeDtypeStruct((b, l, U_DIM), BF16),
                   jax.ShapeDtypeStruct((b, 2 * SSD_HEADS, l), F32)),
        grid=(b, l // tm + 1),
        in_specs=_outproj_specs(l, d, tm) + [
            pl.BlockSpec((None, 1, d), lambda i, j: (i, 0, 0)),
            pl.BlockSpec((None, 1, d), lambda i, j: (i, 0, 0)),
            pl.BlockSpec((1, d), lambda i, j: (0, 0)),
            pl.BlockSpec((d, U_DIM), lambda i, j: (0, 0)),
            pl.BlockSpec((d, DT_PAD), lambda i, j: (0, 0)),
        ],
        out_specs=(
            pl.BlockSpec((None, tm, d), lambda i, j: (i, jnp.minimum(j, last), 0)),
            pl.BlockSpec((None, tm, U_DIM), lambda i, j: (i, jnp.maximum(j - 1, 0), 0)),
            pl.BlockSpec((None, 2 * SSD_HEADS, tm), lambda i, j: (i, 0, jnp.maximum(j - 1, 0))),
        ),
        scratch_shapes=[pltpu.VMEM((tm, DT_PAD), F32), pltpu.VMEM((tm, d), F32)],
        compiler_params=_params("arbitrary", "arbitrary"),
        name="layer_boundary",
    )(u, u, u, y_na, u, y_ss, x, gate, g_post, conv_w, w_out, shift, scale, g_pre, w_u, w_dt)


def kernel(x, c, ctx, c_ctx, w_ada, b_ada, g_pre, g_post, w_in, conv_a_w, rpb, ssd_conv_w, ssd_conv_b,
           dt_bias_f, dt_bias_b, a_log_f, a_log_b, ssd_d, ssd_norm_w, w_out):
    depth = w_ada.shape[0]
    bsz, seq, d = x.shape
    n_ctx = ctx.shape[1]
    assert d == D_MODEL and seq % (2 * ATTN_ROWS * GRID_W) == 0 and seq // GRID_W >= ATTN_KEY_ROWS
    assert n_ctx % SSD_CHUNK == 0 and w_in.shape[-1] == U_DIM + 2 * SSD_HEADS
    tm_lat = min(512, seq)
    tm_ctx = n_ctx

    n_rows = -(-(bsz + 1) // 8) * 8
    cc = jnp.zeros((n_rows, d), F32).at[:bsz].set(c).at[bsz].set(c_ctx)
    mods = _ada_all_layers(cc, w_ada, b_ada)
    cos_t, sin_t = _rope_tables(seq)
    pad8 = DT_PAD - 2 * SSD_HEADS

    def layer_inputs(layer):
        m = mods[layer]
        lat = tuple(m[:bsz, k * d:(k + 1) * d].reshape(bsz, 1, d) for k in range(3))
        ctx_mod = tuple(jnp.broadcast_to(m[bsz, k * d:(k + 1) * d], (bsz, 1, d)) for k in range(3))
        w_u = w_in[layer, :, :U_DIM].astype(BF16)
        w_dt = jnp.pad(w_in[layer, :, U_DIM:], ((0, 0), (0, pad8))).astype(BF16)
        return lat, ctx_mod, w_u, w_dt, g_pre[layer].reshape(1, d)

    x_ctx = ctx
    (shift, scale, gate), (shift_c, scale_c, gate_c), w_u, w_dt, gpre = layer_inputs(0)
    u_lat, dt_lat = _inproj(x, shift, scale, gpre, w_u, w_dt, tm_lat)
    for layer in range(depth):
        gpost = g_post[layer].reshape(1, d)
        per_head = lambda f, bwd: jnp.broadcast_to(jnp.concatenate([f, bwd])[:, None], (2 * SSD_HEADS, SSD_CHUNK))
        dt_bias = per_head(dt_bias_f[layer], dt_bias_b[layer])
        a_log = per_head(a_log_f[layer], a_log_b[layer])
        d_vec = jnp.repeat(ssd_d[layer], SSD_HEAD_DIM).reshape(1, SSD_DIM)
        bias = _attn_bias_table(rpb[layer])

        u_ctx, dt_ctx = _inproj(x_ctx, shift_c, scale_c, gpre, w_u, w_dt, tm_ctx)
        y_na = _attention(u_lat, u_ctx, bias)
        y_ss, y_ss_ctx = _ssd(u_lat, dt_lat, u_ctx, dt_ctx, ssd_conv_w[layer], ssd_conv_b[layer].reshape(1, -1),
                              dt_bias, a_log, d_vec, ssd_norm_w[layer].reshape(1, -1), cos_t, sin_t)
        w_o = w_out[layer].astype(BF16)
        if layer == depth - 1:
            return _outproj(u_lat, y_na, y_ss, x, gate, gpost, conv_a_w[layer], w_o, tm_lat)
        y_na_ctx = _ctx_attention(u_ctx)
        x_ctx = _outproj(u_ctx, y_na_ctx, y_ss_ctx, x_ctx, gate_c, gpost, conv_a_w[layer], w_o, tm_ctx)
        (shift, scale, gate_next), (shift_c, scale_c, gate_c_next), w_u, w_dt, gpre = layer_inputs(layer + 1)
        x, u_lat, dt_lat = _layer_boundary(u_lat, y_na, y_ss, x, gate, gpost, conv_a_w[layer], w_o,
                                           shift, scale, gpre, w_u, w_dt, tm_lat)
        gate, gate_c = gate_next, gate_c_next
```

```python
import functools
import math

import jax
import jax.numpy as jnp
import numpy as np
from jax import lax
from jax.experimental import pallas as pl
from jax.experimental.pallas import tpu as pltpu

F32 = jnp.float32
BF16 = jnp.bfloat16
HIGHEST = lax.Precision.HIGHEST

D_MODEL = 1024
GRID_W = 64
EPS = 1e-6
CONV_DIM = 256
NA_HEADS = 8
NA_HEAD_DIM = 64
NA_DIM = NA_HEADS * NA_HEAD_DIM
WIN_H = 8
WIN_W = 16
SSD_HEADS = 4
SSD_HEAD_DIM = 64
SSD_DIM = SSD_HEADS * SSD_HEAD_DIM
SSD_GROUPS = 2
SSD_STATE = 128
SSD_CHUNK = 128
XBC_DIM = SSD_DIM + 2 * SSD_GROUPS * SSD_STATE
ROPE_BASE = 10000.0
U_DIM = 4 * CONV_DIM + 4 * NA_DIM + XBC_DIM + SSD_DIM
DT_PAD = 128
COL_Q = 4 * CONV_DIM
COL_K = COL_Q + NA_DIM
COL_V = COL_K + NA_DIM
COL_ZNA = COL_V + NA_DIM
COL_XBC = COL_ZNA + NA_DIM
COL_ZSS = COL_XBC + XBC_DIM

LANE = 128
ATTN_ROWS = 4
ATTN_KEY_ROWS = 12
ATTN_STRIP = 16
SSD_UNROLL_FWD = 4
SSD_UNROLL_BWD = 8
VMEM_LIMIT = 56 * 1024 * 1024

NEG_INF = float("-inf")


def _sigmoid(x):
    return 1.0 / (1.0 + jnp.exp(-x))


def _silu(x):
    return x * _sigmoid(x)


def _softplus(x):
    return jnp.maximum(x, 0.0) + jnp.log1p(jnp.exp(-jnp.abs(x)))


def _rms(x):
    return x * lax.rsqrt(jnp.mean(x * x, axis=-1, keepdims=True) + EPS)


def _params(*sem):
    return pltpu.CompilerParams(dimension_semantics=sem, vmem_limit_bytes=VMEM_LIMIT)


def _ada_kernel(cc_ref, w_ref, b_ref, o_ref):
    s = _silu(cc_ref[...])
    o_ref[...] = jnp.dot(s, w_ref[...], preferred_element_type=F32, precision=HIGHEST) + b_ref[...]


def _ada_all_layers(cc, w_ada, b_ada):
    depth = w_ada.shape[0]
    r = cc.shape[0]
    nblk = 3 * D_MODEL // D_MODEL
    return pl.pallas_call(
        _ada_kernel,
        out_shape=jax.ShapeDtypeStruct((depth, r, 3 * D_MODEL), F32),
        grid=(depth, nblk),
        in_specs=[
            pl.BlockSpec((r, D_MODEL), lambda l, n: (0, 0)),
            pl.BlockSpec((None, D_MODEL, D_MODEL), lambda l, n: (l, 0, n)),
            pl.BlockSpec((None, 1, D_MODEL), lambda l, n: (l, 0, n)),
        ],
        out_specs=pl.BlockSpec((None, r, D_MODEL), lambda l, n: (l, 0, n)),
        compiler_params=_params("arbitrary", "arbitrary"),
        name="ada",
    )(cc, w_ada, b_ada.reshape(depth, 1, 3 * D_MODEL))


def _inproj_tile(x, shift_ref, scale_ref, g_ref, w_ref, wdt_ref, u_ref, dt_ref, dt_scr):
    h = _rms(x) * g_ref[...]
    h = h * (1.0 + scale_ref[...]) + shift_ref[...]
    hb = h.astype(BF16)
    nb = 512
    for n in range(0, U_DIM, nb):
        u_ref[:, n:n + nb] = jnp.dot(hb, w_ref[:, n:n + nb], preferred_element_type=F32).astype(BF16)
    dt_scr[...] = jnp.dot(hb, wdt_ref[...], preferred_element_type=F32)
    dt_ref[...] = dt_scr[...].T[0:2 * SSD_HEADS, :]


def _inproj_kernel(x_ref, shift_ref, scale_ref, g_ref, w_ref, wdt_ref, u_ref, dt_ref, dt_scr):
    _inproj_tile(x_ref[...], shift_ref, scale_ref, g_ref, w_ref, wdt_ref, u_ref, dt_ref, dt_scr)


def _inproj(x, shift, scale, g_pre, w_u, w_dt, tm):
    b, l, d = x.shape
    return pl.pallas_call(
        _inproj_kernel,
        out_shape=(jax.ShapeDtypeStruct((b, l, U_DIM), BF16), jax.ShapeDtypeStruct((b, 2 * SSD_HEADS, l), F32)),
        grid=(b, l // tm),
        in_specs=[
            pl.BlockSpec((None, tm, d), lambda i, j: (i, j, 0)),
            pl.BlockSpec((None, 1, d), lambda i, j: (i, 0, 0)),
            pl.BlockSpec((None, 1, d), lambda i, j: (i, 0, 0)),
            pl.BlockSpec((1, d), lambda i, j: (0, 0)),
            pl.BlockSpec((d, U_DIM), lambda i, j: (0, 0)),
            pl.BlockSpec((d, DT_PAD), lambda i, j: (0, 0)),
        ],
        out_specs=(
            pl.BlockSpec((None, tm, U_DIM), lambda i, j: (i, j, 0)),
            pl.BlockSpec((None, 2 * SSD_HEADS, tm), lambda i, j: (i, 0, j)),
        ),
        scratch_shapes=[pltpu.VMEM((tm, DT_PAD), F32)],
        compiler_params=_params("arbitrary", "arbitrary"),
        name="inproj",
    )(x, shift, scale, g_pre, w_u, w_dt)


def _attn_variants(n_rows):
    return {
        "top": dict(jlo=[0, 0, 0, 0], droff=0),
        "mid": dict(jlo=[0, 1, 2, 3], droff=-WIN_H // 2),
        "bot": dict(jlo=[4, 4, 4, 4], droff=-WIN_H),
    }


def _head_scales():
    lane1 = lax.broadcasted_iota(jnp.int32, (1, LANE), 1)
    scale = NA_HEAD_DIM ** -0.5
    lo = jnp.where(lane1 < NA_HEAD_DIM, scale, 0.0).astype(BF16)
    hi = jnp.where(lane1 >= NA_HEAD_DIM, scale, 0.0).astype(BF16)
    return lo, hi


def _softmax_strip(s_scr, p_scr, l_scr, bias_ref, hh, row0, blocks, n_ctx_blocks, n_win_blocks):
    rows = slice(row0, row0 + ATTN_STRIP)
    brow = (row0 % GRID_W)
    lane = lax.broadcasted_iota(jnp.int32, (ATTN_STRIP, LANE), 1)
    vals = []
    for m, d, keep in blocks:
        s = s_scr[rows, m * LANE:(m + 1) * LANE] + bias_ref[hh, d, brow:brow + ATTN_STRIP, :]
        if keep == "lo":
            s = jnp.where(lane < GRID_W, s, NEG_INF)
        elif keep == "hi":
            s = jnp.where(lane >= GRID_W, s, NEG_INF)
        vals.append((m, s))
    for c in range(n_ctx_blocks):
        m = n_win_blocks + c
        vals.append((m, s_scr[rows, m * LANE:(m + 1) * LANE]))
    mx = vals[0][1]
    for _, s in vals[1:]:
        mx = jnp.maximum(mx, s)
    mx = jnp.max(mx, axis=-1, keepdims=True)
    tot = None
    used = set()
    for m, s in vals:
        p = jnp.exp(s - mx)
        tot = p if tot is None else tot + p
        p_scr[rows, m * LANE:(m + 1) * LANE] = p.astype(BF16)
        used.add(m)
    for m in range(n_win_blocks):
        if m not in used:
            p_scr[rows, m * LANE:(m + 1) * LANE] = jnp.zeros((ATTN_STRIP, LANE), BF16)
    l_scr[rows, :] = jnp.broadcast_to(jnp.sum(tot, axis=-1, keepdims=True), (ATTN_STRIP, LANE))


def _attn_kernel(q_ref, k_ref, v_ref, kc_ref, vc_ref, bias_ref, o_ref, s0, s1, p0, p1, l0, l1, *, n_rows):
    n_units = n_rows // ATTN_ROWS
    n_q = ATTN_ROWS * GRID_W
    n_win = ATTN_KEY_ROWS * GRID_W
    n_ctx = kc_ref.shape[0]
    n_win_blocks = n_win // LANE
    n_ctx_blocks = n_ctx // LANE
    variants = _attn_variants(n_rows)
    lane_q = lax.broadcasted_iota(jnp.int32, (n_q, LANE), 1)
    head_scale = _head_scales()
    contract_last = (((1,), (1,)), ((), ()))
    s_bufs, p_bufs, l_bufs = (s0, s1), (p0, p1), (l0, l1)

    def key_start(g):
        kb = jnp.clip(g * ATTN_ROWS - WIN_H // 2, 0, n_rows - ATTN_KEY_ROWS)
        return pl.multiple_of(kb * GRID_W, GRID_W)

    def scores(g, slot):
        s_scr = s_bufs[slot]
        q = q_ref[pl.ds(pl.multiple_of(g * n_q, n_q), n_q), :]
        kw = k_ref[pl.ds(key_start(g), n_win), :]
        for hh in range(2):
            qh = q * head_scale[hh]
            hr = slice(hh * n_q, (hh + 1) * n_q)
            s_scr[hr, 0:n_win] = lax.dot_general(qh, kw, contract_last, preferred_element_type=F32)
            s_scr[hr, n_win:n_win + n_ctx] = lax.dot_general(qh, kc_ref[...], contract_last,
                                                             preferred_element_type=F32)

    def softmax(kind, slot):
        geo = variants[kind]
        n_parts = GRID_W // ATTN_STRIP
        for hh in range(2):
            for i in range(ATTN_ROWS):
                jlo = geo["jlo"][i]
                blocks = []
                for m in range(n_win_blocks):
                    jl, jr = 2 * m, 2 * m + 1
                    vl = jlo <= jl < jlo + WIN_H
                    vr = jlo <= jr < jlo + WIN_H
                    if not (vl or vr):
                        continue
                    d = (jl - i + geo["droff"]) + WIN_H
                    blocks.append((m, d, None if (vl and vr) else ("lo" if vl else "hi")))
                for part in range(n_parts):
                    row0 = hh * n_q + i * GRID_W + part * ATTN_STRIP
                    _softmax_strip(s_bufs[slot], p_bufs[slot], l_bufs[slot], bias_ref, hh, row0, blocks,
                                   n_ctx_blocks, n_win_blocks)

    def pv(g, slot):
        p_scr = p_bufs[slot]
        vw = v_ref[pl.ds(key_start(g), n_win), :]
        o2 = jnp.dot(p_scr[:, 0:n_win], vw, preferred_element_type=F32)
        o2 = o2 + jnp.dot(p_scr[:, n_win:n_win + n_ctx], vc_ref[...], preferred_element_type=F32)
        o2 = o2 * (1.0 / l_bufs[slot][...])
        out = jnp.where(lane_q < GRID_W, o2[0:n_q], o2[n_q:2 * n_q])
        o_ref[pl.ds(pl.multiple_of(g * n_q, n_q), n_q), :] = out.astype(o_ref.dtype)

    scores(jnp.int32(0), 0)
    scores(jnp.int32(1), 1)
    softmax("top", 0)

    def pair_body(tt, carry):
        t = 1 + 2 * tt
        softmax("mid", 1)
        pv(t - 1, 0)
        scores(t + 1, 0)
        softmax("mid", 0)
        pv(t, 1)
        scores(t + 2, 1)
        return carry

    lax.fori_loop(0, (n_units - 2) // 2, pair_body, 0, unroll=True)
    softmax("bot", 1)
    pv(jnp.int32(n_units - 2), 0)
    pv(jnp.int32(n_units - 1), 1)


def _attention(u, u_ctx, bias):
    b, l, _ = u.shape
    n_ctx = u_ctx.shape[1]
    n_rows = l // GRID_W
    n_pairs = NA_HEADS // 2
    n_q = ATTN_ROWS * GRID_W
    n_keys = ATTN_KEY_ROWS * GRID_W + n_ctx
    qb, kb, vb = COL_Q // LANE, COL_K // LANE, COL_V // LANE
    return pl.pallas_call(
        functools.partial(_attn_kernel, n_rows=n_rows),
        out_shape=jax.ShapeDtypeStruct((b, l, NA_DIM), BF16),
        grid=(b, n_pairs),
        in_specs=[
            pl.BlockSpec((None, l, LANE), lambda i, p: (i, 0, qb + p)),
            pl.BlockSpec((None, l, LANE), lambda i, p: (i, 0, kb + p)),
            pl.BlockSpec((None, l, LANE), lambda i, p: (i, 0, vb + p)),
            pl.BlockSpec((None, n_ctx, LANE), lambda i, p: (i, 0, kb + p)),
            pl.BlockSpec((None, n_ctx, LANE), lambda i, p: (i, 0, vb + p)),
            pl.BlockSpec((2, 2 * WIN_H, GRID_W, LANE), lambda i, p: (p, 0, 0, 0)),
        ],
        out_specs=pl.BlockSpec((None, l, LANE), lambda i, p: (i, 0, p)),
        scratch_shapes=[
            pltpu.VMEM((2 * n_q, n_keys), F32),
            pltpu.VMEM((2 * n_q, n_keys), F32),
            pltpu.VMEM((2 * n_q, n_keys), BF16),
            pltpu.VMEM((2 * n_q, n_keys), BF16),
            pltpu.VMEM((2 * n_q, LANE), F32),
            pltpu.VMEM((2 * n_q, LANE), F32),
        ],
        compiler_params=_params("arbitrary", "arbitrary"),
        name="natten",
    )(u, u, u, u_ctx, u_ctx, bias)


def _ctx_attn_kernel(q_ref, k_ref, v_ref, o_ref):
    n_q = q_ref.shape[0]
    lane_q = lax.broadcasted_iota(jnp.int32, (n_q, LANE), 1)
    head_scale = _head_scales()
    q = q_ref[...]
    q2 = jnp.concatenate([q * head_scale[0], q * head_scale[1]], axis=0)
    s = lax.dot_general(q2, k_ref[...], (((1,), (1,)), ((), ())), preferred_element_type=F32)
    p = jnp.exp(s - jnp.max(s, axis=-1, keepdims=True))
    rinv = 1.0 / jnp.sum(p, axis=-1, keepdims=True)
    o2 = jnp.dot(p.astype(BF16), v_ref[...], preferred_element_type=F32) * rinv
    o_ref[...] = jnp.where(lane_q < GRID_W, o2[0:n_q], o2[n_q:2 * n_q]).astype(o_ref.dtype)


def _ctx_attention(u_ctx):
    b, n_ctx, _ = u_ctx.shape
    qb, kb, vb = COL_Q // LANE, COL_K // LANE, COL_V // LANE
    return pl.pallas_call(
        _ctx_attn_kernel,
        out_shape=jax.ShapeDtypeStruct((b, n_ctx, NA_DIM), BF16),
        grid=(b, NA_HEADS // 2),
        in_specs=[
            pl.BlockSpec((None, n_ctx, LANE), lambda i, p: (i, 0, qb + p)),
            pl.BlockSpec((None, n_ctx, LANE), lambda i, p: (i, 0, kb + p)),
            pl.BlockSpec((None, n_ctx, LANE), lambda i, p: (i, 0, vb + p)),
        ],
        out_specs=pl.BlockSpec((None, n_ctx, LANE), lambda i, p: (i, 0, p)),
        compiler_params=_params("arbitrary", "arbitrary"),
        name="ctx_attn",
    )(u_ctx, u_ctx, u_ctx)


def _attn_bias_table(rpb):
    heads = rpb.shape[0]
    cols = np.arange(GRID_W)
    c0 = np.clip(cols - WIN_W // 2, 0, GRID_W - WIN_W)
    rel = cols[None, :] - cols[:, None] + (WIN_W - 1)
    inside = (cols[None, :] >= c0[:, None]) & (cols[None, :] < c0[:, None] + WIN_W)
    select = ((rel[None] == np.arange(2 * WIN_W - 1)[:, None, None]) & inside[None]).astype(np.float32)
    blk = jnp.einsum("hrj,jck->hrck", rpb.astype(F32), jnp.asarray(select), precision=HIGHEST)
    blk = jnp.where(jnp.asarray(inside)[None, None], blk, NEG_INF)
    pad = jnp.full((heads, 1, GRID_W, GRID_W), NEG_INF, F32)
    left = jnp.concatenate([pad, blk], axis=1)
    right = jnp.concatenate([blk, pad], axis=1)
    return jnp.concatenate([left, right], axis=-1).astype(F32)


def _ssd_kernel(xl_ref, zl_ref, dtl_ref, xc_ref, zc_ref, dtc_ref, cw_ref, cb_ref, dtb_ref, alog_ref,
                dvec_ref, nw_ref, cos_ref, sin_ref, yl_ref, yc_ref,
                xs_l, bc_l, xs_c, bc_c, yacc_l, yacc_c):
    q = SSD_CHUNK
    ri = lax.broadcasted_iota(jnp.int32, (q, q), 0)
    ci = lax.broadcasted_iota(jnp.int32, (q, q), 1)
    lane = lax.broadcasted_iota(jnp.int32, (q, LANE), 1)
    lane1 = lax.broadcasted_iota(jnp.int32, (1, LANE), 1)
    rows8 = lax.broadcasted_iota(jnp.int32, (8, 1), 0)
    lo_half = lane < SSD_HEAD_DIM
    lo_half1 = lane1 < SSD_HEAD_DIM
    a_all = -jnp.exp(alog_ref[...])
    n_hd = 2 * SSD_HEADS

    def prep(raw_ref, c, n_chunks, rope, xs_ref, bc_ref):
        base = pl.multiple_of(c * q, q)
        seq = n_chunks * q
        x = raw_ref[pl.ds(base, q), :].astype(F32)
        pstart = pl.multiple_of(jnp.maximum(base - 16, 0), 16)
        nstart = pl.multiple_of(jnp.minimum(base + q, seq - 16), 16)
        prev = raw_ref[pl.ds(pstart, 16), :][15:16, :].astype(F32) * jnp.where(c > 0, 1.0, 0.0)
        nxt = raw_ref[pl.ds(nstart, 16), :][0:1, :].astype(F32) * jnp.where(c < n_chunks - 1, 1.0, 0.0)
        xm1 = pltpu.roll(x, 1, 0)
        xm1 = jnp.concatenate([jnp.where(rows8 == 0, prev, xm1[0:8]), xm1[8:]], axis=0)
        xp1 = pltpu.roll(x, q - 1, 0)
        xp1 = jnp.concatenate([xp1[:q - 8], jnp.where(rows8 == 7, nxt, xp1[q - 8:])], axis=0)
        y = xm1 * cw_ref[0:1, :] + x * cw_ref[1:2, :] + xp1 * cw_ref[2:3, :] + cb_ref[...]
        y = _silu(y)
        xs_ref[pl.ds(base, q), :] = y[:, 0:SSD_DIM]
        for t in range(2 * SSD_GROUPS):
            blk = y[:, SSD_DIM + t * SSD_STATE:SSD_DIM + (t + 1) * SSD_STATE]
            if rope:
                sw = jnp.where(jnp.bitwise_and(lane, 63) < 32, pltpu.roll(blk, 96, 1), pltpu.roll(blk, 32, 1))
                blk = blk * cos_ref[pl.ds(base, q), :] + sw * sin_ref[pl.ds(base, q), :]
            bc_ref[pl.ds(base, q), t * SSD_STATE:(t + 1) * SSD_STATE] = blk.astype(BF16)

    def scan_chunk(c, direction, dt_ref, xs_ref, bc_ref, states):
        base = pl.multiple_of(c * q, q)
        dt_t = _softplus(dt_ref[:, pl.ds(base, q)] + dtb_ref[...])
        tri = jnp.where(ri <= ci, 1.0, 0.0) if direction == 0 else jnp.where(ri >= ci, 1.0, 0.0)
        acs_t = jnp.dot(dt_t * a_all, tri.astype(F32), preferred_element_type=F32, precision=HIGHEST)
        stacked = jnp.concatenate([dt_t, acs_t, jnp.zeros((q - 2 * n_hd, q), F32)], axis=0)
        cols_all = stacked.T
        mask = (ri >= ci) if direction == 0 else (ri <= ci)
        end = q - 1 if direction == 0 else 0
        ys = []
        new_states = []
        for g in range(SSD_GROUPS):
            xg = xs_ref[pl.ds(base, q), g * LANE:(g + 1) * LANE]
            bg = bc_ref[pl.ds(base, q), g * SSD_STATE:(g + 1) * SSD_STATE]
            cg = bc_ref[pl.ds(base, q), (SSD_GROUPS + g) * SSD_STATE:(SSD_GROUPS + g + 1) * SSD_STATE]
            gram = lax.dot_general(cg, bg, (((1,), (1,)), ((), ())), preferred_element_type=F32)
            heads = [direction * SSD_HEADS + 2 * g, direction * SSD_HEADS + 2 * g + 1]
            a_bc = [jnp.broadcast_to(cols_all[:, n_hd + k:n_hd + k + 1], (q, LANE)) for k in heads]
            d_bc = [jnp.broadcast_to(cols_all[:, k:k + 1], (q, LANE)) for k in heads]
            a_row = [acs_t[k:k + 1, :] for k in heads]
            a_end = [acs_t[k:k + 1, end:end + 1] for k in heads]
            dtp = jnp.where(lo_half, d_bc[0], d_bc[1])
            acp = jnp.where(lo_half, a_bc[0], a_bc[1])
            a_end_p = jnp.where(lo_half1, a_end[0], a_end[1])
            xdt = (xg * dtp).astype(BF16)
            yd = []
            for k in range(2):
                decay = jnp.exp(jnp.where(mask, a_bc[k] - a_row[k], NEG_INF))
                yd.append(jnp.dot((gram * decay).astype(BF16), xdt, preferred_element_type=F32))
            y_diag = jnp.where(lo_half, yd[0], yd[1])
            st = states[g]
            y_off = jnp.dot(cg, st.astype(BF16), preferred_element_type=F32) * jnp.exp(acp)
            xw = (xg * (jnp.exp(a_end_p - acp) * dtp)).astype(BF16)
            bgt = bg.astype(F32).T.astype(BF16)
            upd = jnp.dot(bgt, xw, preferred_element_type=F32)
            new_states.append(st * jnp.exp(a_end_p) + upd)
            ys.append(y_diag + y_off)
        return jnp.concatenate(ys, axis=-1), tuple(new_states)

    def finish(y, c, xs_ref, z_ref, out_ref):
        base = pl.multiple_of(c * q, q)
        y = y + dvec_ref[...] * xs_ref[pl.ds(base, q), :]
        y = y * _silu(z_ref[pl.ds(base, q), :].astype(F32))
        out_ref[pl.ds(base, q), :] = (_rms(y) * nw_ref[...]).astype(out_ref.dtype)

    n_c = xc_ref.shape[0] // q
    n_l = xl_ref.shape[0] // q

    def fwd_pass(raw_ref, dt_ref, n_chunks, rope, xs_ref, bc_ref, yacc_ref, states):
        def body(c, states):
            prep(raw_ref, c, n_chunks, rope, xs_ref, bc_ref)
            base = pl.multiple_of(c * q, q)
            y, states = scan_chunk(c, 0, dt_ref, xs_ref, bc_ref, states)
            yacc_ref[pl.ds(base, q), :] = y
            return states

        return lax.fori_loop(0, n_chunks, body, states, unroll=min(n_chunks, SSD_UNROLL_FWD))

    def bwd_pass(dt_ref, n_chunks, xs_ref, bc_ref, yacc_ref, z_ref, out_ref, states):
        def body(t, states):
            c = n_chunks - 1 - t
            base = pl.multiple_of(c * q, q)
            y, states = scan_chunk(c, 1, dt_ref, xs_ref, bc_ref, states)
            finish(y + yacc_ref[pl.ds(base, q), :], c, xs_ref, z_ref, out_ref)
            return states

        return lax.fori_loop(0, n_chunks, body, states, unroll=min(n_chunks, SSD_UNROLL_BWD))

    zero_states = tuple(jnp.zeros((SSD_STATE, LANE), F32) for _ in range(SSD_GROUPS))
    states = fwd_pass(xc_ref, dtc_ref, n_c, False, xs_c, bc_c, yacc_c, zero_states)
    fwd_pass(xl_ref, dtl_ref, n_l, True, xs_l, bc_l, yacc_l, states)
    states = bwd_pass(dtc_ref, n_c, xs_c, bc_c, yacc_c, zc_ref, yc_ref, zero_states)
    bwd_pass(dtl_ref, n_l, xs_l, bc_l, yacc_l, zl_ref, yl_ref, states)


def _ssd(u, dt, u_ctx, dt_ctx, conv_w, conv_b, dt_bias, a_log, d_vec, norm_w, cos_t, sin_t):
    b, l, _ = u.shape
    n_ctx = u_ctx.shape[1]
    xb = COL_XBC // XBC_DIM
    zb = COL_ZSS // SSD_DIM
    n_bc = 2 * SSD_GROUPS * SSD_STATE
    const = lambda shape: pl.BlockSpec(shape, lambda i: (0,) * len(shape))
    return pl.pallas_call(
        _ssd_kernel,
        out_shape=(jax.ShapeDtypeStruct((b, l, SSD_DIM), BF16), jax.ShapeDtypeStruct((b, n_ctx, SSD_DIM), BF16)),
        grid=(b,),
        in_specs=[
            pl.BlockSpec((None, l, XBC_DIM), lambda i: (i, 0, xb)),
            pl.BlockSpec((None, l, SSD_DIM), lambda i: (i, 0, zb)),
            pl.BlockSpec((None, 2 * SSD_HEADS, l), lambda i: (i, 0, 0)),
            pl.BlockSpec((None, n_ctx, XBC_DIM), lambda i: (i, 0, xb)),
            pl.BlockSpec((None, n_ctx, SSD_DIM), lambda i: (i, 0, zb)),
            pl.BlockSpec((None, 2 * SSD_HEADS, n_ctx), lambda i: (i, 0, 0)),
            const((3, XBC_DIM)),
            const((1, XBC_DIM)),
            const((2 * SSD_HEADS, SSD_CHUNK)),
            const((2 * SSD_HEADS, SSD_CHUNK)),
            const((1, SSD_DIM)),
            const((1, SSD_DIM)),
            const((l, SSD_STATE)),
            const((l, SSD_STATE)),
        ],
        out_specs=(
            pl.BlockSpec((None, l, SSD_DIM), lambda i: (i, 0, 0)),
            pl.BlockSpec((None, n_ctx, SSD_DIM), lambda i: (i, 0, 0)),
        ),
        scratch_shapes=[
            pltpu.VMEM((l, SSD_DIM), F32),
            pltpu.VMEM((l, n_bc), BF16),
            pltpu.VMEM((n_ctx, SSD_DIM), F32),
            pltpu.VMEM((n_ctx, n_bc), BF16),
            pltpu.VMEM((l, SSD_DIM), F32),
            pltpu.VMEM((n_ctx, SSD_DIM), F32),
        ],
        compiler_params=_params("arbitrary"),
        name="ssd",
    )(u, u, dt, u_ctx, u_ctx, dt_ctx, conv_w, conv_b, dt_bias, a_log, d_vec, norm_w, cos_t, sin_t)


def _rope_tables(seq):
    n_freq = SSD_STATE // 4
    pos = jnp.arange(seq)
    row_pos = (pos // GRID_W).astype(F32)
    col_pos = (pos % GRID_W).astype(F32)
    inv_freq = ROPE_BASE ** (-jnp.arange(n_freq, dtype=F32) / n_freq)
    ar = row_pos[:, None] * inv_freq
    ac = col_pos[:, None] * inv_freq
    cos_t = jnp.concatenate([jnp.cos(ar), jnp.cos(ar), jnp.cos(ac), jnp.cos(ac)], axis=-1)
    sin_t = jnp.concatenate([-jnp.sin(ar), jnp.sin(ar), -jnp.sin(ac), jnp.sin(ac)], axis=-1)
    return cos_t, sin_t


def _outproj_tile(j, nt, ua_ref, up_ref, un_ref, yna_ref, zna_ref, yss_ref, x_ref, gate_ref, gpost_ref, cw_ref,
                  w_ref):
    tm = ua_ref.shape[0]
    cd = CONV_DIM
    rows8 = lax.broadcasted_iota(jnp.int32, (8, 1), 0)
    ua = ua_ref[...]
    t = ua[:, 2 * cd:3 * cd].astype(F32) * ua[:, 0:cd].astype(F32)
    up = up_ref[15:16, :]
    un = un_ref[0:1, :]
    tprev = up[:, 2 * cd:3 * cd].astype(F32) * up[:, 0:cd].astype(F32) * (j > 0).astype(F32)
    tnext = un[:, 2 * cd:3 * cd].astype(F32) * un[:, 0:cd].astype(F32) * (j < nt - 1).astype(F32)
    tm1 = pltpu.roll(t, 1, 0)
    tm1 = jnp.concatenate([jnp.where(rows8 == 0, tprev, tm1[0:8]), tm1[8:]], axis=0)
    tp1 = pltpu.roll(t, tm - 1, 0)
    tp1 = jnp.concatenate([tp1[:tm - 8], jnp.where(rows8 == 7, tnext, tp1[tm - 8:])], axis=0)
    conv = tm1 * cw_ref[0:1, :] + t * cw_ref[1:2, :] + tp1 * cw_ref[2:3, :]
    ysc = ua[:, cd:2 * cd].astype(F32) * conv * _silu(ua[:, 3 * cd:4 * cd].astype(F32))
    yna = yna_ref[...].astype(F32) * _silu(zna_ref[...].astype(F32))
    ycat = jnp.concatenate([ysc.astype(BF16), yna.astype(BF16), yss_ref[...]], axis=-1)
    out = jnp.dot(ycat, w_ref[...], preferred_element_type=F32)
    return x_ref[...] + gate_ref[...] * (_rms(out) * gpost_ref[...])


def _outproj_kernel(*refs):
    refs[-1][...] = _outproj_tile(pl.program_id(1), pl.num_programs(1), *refs[:-1])


N_OUTPROJ_IN = 11


def _layer_boundary_kernel(*refs):
    out_in = refs[:N_OUTPROJ_IN]
    shift_ref, scale_ref, g_ref, w_ref, wdt_ref = refs[N_OUTPROJ_IN:N_OUTPROJ_IN + 5]
    x_out_ref, u_ref, dt_ref, dt_scr, x_keep = refs[N_OUTPROJ_IN + 5:]
    j = pl.program_id(1)
    nt = pl.num_programs(1) - 1

    def in_half():
        _inproj_tile(x_keep[...], shift_ref, scale_ref, g_ref, w_ref, wdt_ref, u_ref, dt_ref, dt_scr)

    def out_half():
        x_new = _outproj_tile(j, nt, *out_in)
        x_out_ref[...] = x_new
        x_keep[...] = x_new

    @pl.when(j == 0)
    def _():
        out_half()

    @pl.when(jnp.logical_and(j > 0, j < nt))
    def _():
        in_half()
        out_half()

    @pl.when(j == nt)
    def _():
        in_half()


def _outproj_specs(l, d, tm):
    hb = tm // 16
    n_hb = l // 16
    zb = COL_ZNA // NA_DIM
    last = l // tm - 1
    tile = lambda j: jnp.minimum(j, last)
    return [
        pl.BlockSpec((None, tm, 4 * CONV_DIM), lambda i, j: (i, tile(j), 0)),
        pl.BlockSpec((None, 16, 4 * CONV_DIM), lambda i, j: (i, jnp.maximum(tile(j) * hb - 1, 0), 0)),
        pl.BlockSpec((None, 16, 4 * CONV_DIM), lambda i, j: (i, jnp.minimum((tile(j) + 1) * hb, n_hb - 1), 0)),
        pl.BlockSpec((None, tm, NA_DIM), lambda i, j: (i, tile(j), 0)),
        pl.BlockSpec((None, tm, NA_DIM), lambda i, j: (i, tile(j), zb)),
        pl.BlockSpec((None, tm, SSD_DIM), lambda i, j: (i, tile(j), 0)),
        pl.BlockSpec((None, tm, d), lambda i, j: (i, tile(j), 0)),
        pl.BlockSpec((None, 1, d), lambda i, j: (i, 0, 0)),
        pl.BlockSpec((1, d), lambda i, j: (0, 0)),
        pl.BlockSpec((3, CONV_DIM), lambda i, j: (0, 0)),
        pl.BlockSpec((d, d), lambda i, j: (0, 0)),
    ]


def _outproj(u, y_na, y_ss, x, gate, g_post, conv_w, w_out, tm):
    b, l, d = x.shape
    return pl.pallas_call(
        _outproj_kernel,
        out_shape=jax.ShapeDtypeStruct((b, l, d), F32),
        grid=(b, l // tm),
        in_specs=_outproj_specs(l, d, tm),
        out_specs=pl.BlockSpec((None, tm, d), lambda i, j: (i, j, 0)),
        compiler_params=_params("arbitrary", "arbitrary"),
        name="outproj",
    )(u, u, u, y_na, u, y_ss, x, gate, g_post, conv_w, w_out)


def _layer_boundary(u, y_na, y_ss, x, gate, g_post, conv_w, w_out, shift, scale, g_pre, w_u, w_dt, tm):
    b, l, d = x.shape
    last = l // tm - 1
    return pl.pallas_call(
        _layer_boundary_kernel,
        out_shape=(jax.ShapeDtypeStruct((b, l, d), F32), jax.ShapeDtypeStruct((b, l, U_DIM), BF16),
                   jax.ShapeDtypeStruct((b, 2 * SSD_HEADS, l), F32)),
        grid=(b, l // tm + 1),
        in_specs=_outproj_specs(l, d, tm) + [
            pl.BlockSpec((None, 1, d), lambda i, j: (i, 0, 0)),
            pl.BlockSpec((None, 1, d), lambda i, j: (i, 0, 0)),
            pl.BlockSpec((1, d), lambda i, j: (0, 0)),
            pl.BlockSpec((d, U_DIM), lambda i, j: (0, 0)),
            pl.BlockSpec((d, DT_PAD), lambda i, j: (0, 0)),
        ],
        out_specs=(
            pl.BlockSpec((None, tm, d), lambda i, j: (i, jnp.minimum(j, last), 0)),
            pl.BlockSpec((None, tm, U_DIM), lambda i, j: (i, jnp.maximum(j - 1, 0), 0)),
            pl.BlockSpec((None, 2 * SSD_HEADS, tm), lambda i, j: (i, 0, jnp.maximum(j - 1, 0))),
        ),
        scratch_shapes=[pltpu.VMEM((tm, DT_PAD), F32), pltpu.VMEM((tm, d), F32)],
        compiler_params=_params("arbitrary", "arbitrary"),
        name="layer_boundary",
    )(u, u, u, y_na, u, y_ss, x, gate, g_post, conv_w, w_out, shift, scale, g_pre, w_u, w_dt)


def kernel(x, c, ctx, c_ctx, w_ada, b_ada, g_pre, g_post, w_in, conv_a_w, rpb, ssd_conv_w, ssd_conv_b,
           dt_bias_f, dt_bias_b, a_log_f, a_log_b, ssd_d, ssd_norm_w, w_out):
    depth = w_ada.shape[0]
    bsz, seq, d = x.shape
    n_ctx = ctx.shape[1]
    assert d == D_MODEL and seq % (2 * ATTN_ROWS * GRID_W) == 0 and seq // GRID_W >= ATTN_KEY_ROWS
    assert n_ctx % SSD_CHUNK == 0 and w_in.shape[-1] == U_DIM + 2 * SSD_HEADS
    tm_lat = min(512, seq)
    tm_ctx = n_ctx

    n_rows = -(-(bsz + 1) // 8) * 8
    cc = jnp.zeros((n_rows, d), F32).at[:bsz].set(c).at[bsz].set(c_ctx)
    mods = _ada_all_layers(cc, w_ada, b_ada)
    cos_t, sin_t = _rope_tables(seq)
    pad8 = DT_PAD - 2 * SSD_HEADS

    def layer_inputs(layer):
        m = mods[layer]
        lat = tuple(m[:bsz, k * d:(k + 1) * d].reshape(bsz, 1, d) for k in range(3))
        ctx_mod = tuple(jnp.broadcast_to(m[bsz, k * d:(k + 1) * d], (bsz, 1, d)) for k in range(3))
        w_u = w_in[layer, :, :U_DIM].astype(BF16)
        w_dt = jnp.pad(w_in[layer, :, U_DIM:], ((0, 0), (0, pad8))).astype(BF16)
        return lat, ctx_mod, w_u, w_dt, g_pre[layer].reshape(1, d)

    x_ctx = ctx
    (shift, scale, gate), (shift_c, scale_c, gate_c), w_u, w_dt, gpre = layer_inputs(0)
    u_lat, dt_lat = _inproj(x, shift, scale, gpre, w_u, w_dt, tm_lat)
    for layer in range(depth):
        gpost = g_post[layer].reshape(1, d)
        per_head = lambda f, bwd: jnp.broadcast_to(jnp.concatenate([f, bwd])[:, None], (2 * SSD_HEADS, SSD_CHUNK))
        dt_bias = per_head(dt_bias_f[layer], dt_bias_b[layer])
        a_log = per_head(a_log_f[layer], a_log_b[layer])
        d_vec = jnp.repeat(ssd_d[layer], SSD_HEAD_DIM).reshape(1, SSD_DIM)
        bias = _attn_bias_table(rpb[layer])

        u_ctx, dt_ctx = _inproj(x_ctx, shift_c, scale_c, gpre, w_u, w_dt, tm_ctx)
        y_na = _attention(u_lat, u_ctx, bias)
        y_ss, y_ss_ctx = _ssd(u_lat, dt_lat, u_ctx, dt_ctx, ssd_conv_w[layer], ssd_conv_b[layer].reshape(1, -1),
                              dt_bias, a_log, d_vec, ssd_norm_w[layer].reshape(1, -1), cos_t, sin_t)
        w_o = w_out[layer].astype(BF16)
        if layer == depth - 1:
            return _outproj(u_lat, y_na, y_ss, x, gate, gpost, conv_a_w[layer], w_o, tm_lat)
        y_na_ctx = _ctx_attention(u_ctx)
        x_ctx = _outproj(u_ctx, y_na_ctx, y_ss_ctx, x_ctx, gate_c, gpost, conv_a_w[layer], w_o, tm_ctx)
        (shift, scale, gate_next), (shift_c, scale_c, gate_c_next), w_u, w_dt, gpre = layer_inputs(layer + 1)
        x, u_lat, dt_lat = _layer_boundary(u_lat, y_na, y_ss, x, gate, gpost, conv_a_w[layer], w_o,
                                           shift, scale, gpre, w_u, w_dt, tm_lat)
        gate, gate_c = gate_next, gate_c_next
```

```python
import functools
import math

import jax
import jax.numpy as jnp
import numpy as np
from jax import lax
from jax.experimental import pallas as pl
from jax.experimental.pallas import tpu as pltpu

F32 = jnp.float32
BF16 = jnp.bfloat16
HIGHEST = lax.Precision.HIGHEST

D_MODEL = 1024
GRID_W = 64
EPS = 1e-6
CONV_DIM = 256
NA_HEADS = 8
NA_HEAD_DIM = 64
NA_DIM = NA_HEADS * NA_HEAD_DIM
WIN_H = 8
WIN_W = 16
SSD_HEADS = 4
SSD_HEAD_DIM = 64
SSD_DIM = SSD_HEADS * SSD_HEAD_DIM
SSD_GROUPS = 2
SSD_STATE = 128
SSD_CHUNK = 128
XBC_DIM = SSD_DIM + 2 * SSD_GROUPS * SSD_STATE
ROPE_BASE = 10000.0
U_DIM = 4 * CONV_DIM + 4 * NA_DIM + XBC_DIM + SSD_DIM
DT_PAD = 128
COL_Q = 4 * CONV_DIM
COL_K = COL_Q + NA_DIM
COL_V = COL_K + NA_DIM
COL_ZNA = COL_V + NA_DIM
COL_XBC = COL_ZNA + NA_DIM
COL_ZSS = COL_XBC + XBC_DIM

LANE = 128
ATTN_ROWS = 4
ATTN_KEY_ROWS = 12
ATTN_STRIP = 16
ATTN_SLOTS = 3
SSD_UNROLL_FWD = 4
SSD_UNROLL_BWD = 8
VMEM_LIMIT = 56 * 1024 * 1024

NEG_INF = float("-inf")


def _sigmoid(x):
    return 1.0 / (1.0 + jnp.exp(-x))


def _silu(x):
    return x * _sigmoid(x)


def _softplus(x):
    return jnp.maximum(x, 0.0) + jnp.log1p(jnp.exp(-jnp.abs(x)))


def _rms(x):
    return x * lax.rsqrt(jnp.mean(x * x, axis=-1, keepdims=True) + EPS)


def _params(*sem):
    return pltpu.CompilerParams(dimension_semantics=sem, vmem_limit_bytes=VMEM_LIMIT)


def _ada_kernel(cc_ref, w_ref, b_ref, o_ref):
    s = _silu(cc_ref[...])
    o_ref[...] = jnp.dot(s, w_ref[...], preferred_element_type=F32, precision=HIGHEST) + b_ref[...]


def _ada_all_layers(cc, w_ada, b_ada):
    depth = w_ada.shape[0]
    r = cc.shape[0]
    nblk = 3 * D_MODEL // D_MODEL
    return pl.pallas_call(
        _ada_kernel,
        out_shape=jax.ShapeDtypeStruct((depth, r, 3 * D_MODEL), F32),
        grid=(depth, nblk),
        in_specs=[
            pl.BlockSpec((r, D_MODEL), lambda l, n: (0, 0)),
            pl.BlockSpec((None, D_MODEL, D_MODEL), lambda l, n: (l, 0, n)),
            pl.BlockSpec((None, 1, D_MODEL), lambda l, n: (l, 0, n)),
        ],
        out_specs=pl.BlockSpec((None, r, D_MODEL), lambda l, n: (l, 0, n)),
        compiler_params=_params("arbitrary", "arbitrary"),
        name="ada",
    )(cc, w_ada, b_ada.reshape(depth, 1, 3 * D_MODEL))


def _prenorm_tile(x, shift_ref, scale_ref, g_ref):
    h = _rms(x) * g_ref[...]
    h = h * (1.0 + scale_ref[...]) + shift_ref[...]
    return h.astype(BF16)


def _project_tile(hb, w_ref, wdt_ref, u_ref, dt_ref, dt_scr):
    nb = 512
    for n in range(0, U_DIM, nb):
        u_ref[:, n:n + nb] = jnp.dot(hb, w_ref[:, n:n + nb], preferred_element_type=F32).astype(BF16)
    dt_scr[...] = jnp.dot(hb, wdt_ref[...], preferred_element_type=F32)
    dt_ref[...] = dt_scr[...].T[0:2 * SSD_HEADS, :]


def _inproj_kernel(x_ref, shift_ref, scale_ref, g_ref, w_ref, wdt_ref, u_ref, dt_ref, dt_scr):
    _project_tile(_prenorm_tile(x_ref[...], shift_ref, scale_ref, g_ref), w_ref, wdt_ref, u_ref, dt_ref, dt_scr)


def _inproj(x, shift, scale, g_pre, w_u, w_dt, tm):
    b, l, d = x.shape
    return pl.pallas_call(
        _inproj_kernel,
        out_shape=(jax.ShapeDtypeStruct((b, l, U_DIM), BF16), jax.ShapeDtypeStruct((b, 2 * SSD_HEADS, l), F32)),
        grid=(b, l // tm),
        in_specs=[
            pl.BlockSpec((None, tm, d), lambda i, j: (i, j, 0)),
            pl.BlockSpec((None, 1, d), lambda i, j: (i, 0, 0)),
            pl.BlockSpec((None, 1, d), lambda i, j: (i, 0, 0)),
            pl.BlockSpec((1, d), lambda i, j: (0, 0)),
            pl.BlockSpec((d, U_DIM), lambda i, j: (0, 0)),
            pl.BlockSpec((d, DT_PAD), lambda i, j: (0, 0)),
        ],
        out_specs=(
            pl.BlockSpec((None, tm, U_DIM), lambda i, j: (i, j, 0)),
            pl.BlockSpec((None, 2 * SSD_HEADS, tm), lambda i, j: (i, 0, j)),
        ),
        scratch_shapes=[pltpu.VMEM((tm, DT_PAD), F32)],
        compiler_params=_params("arbitrary", "arbitrary"),
        name="inproj",
    )(x, shift, scale, g_pre, w_u, w_dt)


def _attn_variants(n_rows):
    return {
        "top": dict(jlo=[0, 0, 0, 0], droff=0),
        "mid": dict(jlo=[0, 1, 2, 3], droff=-WIN_H // 2),
        "bot": dict(jlo=[4, 4, 4, 4], droff=-WIN_H),
    }


def _head_scales():
    lane1 = lax.broadcasted_iota(jnp.int32, (1, LANE), 1)
    scale = NA_HEAD_DIM ** -0.5
    lo = jnp.where(lane1 < NA_HEAD_DIM, scale, 0.0).astype(BF16)
    hi = jnp.where(lane1 >= NA_HEAD_DIM, scale, 0.0).astype(BF16)
    return lo, hi


def _softmax_strip(s_scr, p_scr, l_scr, bias_ref, hh, row0, blocks, n_ctx_blocks, n_win_blocks):
    rows = slice(row0, row0 + ATTN_STRIP)
    brow = (row0 % GRID_W)
    lane = lax.broadcasted_iota(jnp.int32, (ATTN_STRIP, LANE), 1)
    vals = []
    for m, d, keep in blocks:
        s = s_scr[rows, m * LANE:(m + 1) * LANE] + bias_ref[hh, d, brow:brow + ATTN_STRIP, :]
        if keep == "lo":
            s = jnp.where(lane < GRID_W, s, NEG_INF)
        elif keep == "hi":
            s = jnp.where(lane >= GRID_W, s, NEG_INF)
        vals.append((m, s))
    for c in range(n_ctx_blocks):
        m = n_win_blocks + c
        vals.append((m, s_scr[rows, m * LANE:(m + 1) * LANE]))
    mx = vals[0][1]
    for _, s in vals[1:]:
        mx = jnp.maximum(mx, s)
    mx = jnp.max(mx, axis=-1, keepdims=True)
    tot = None
    used = set()
    for m, s in vals:
        p = jnp.exp(s - mx)
        tot = p if tot is None else tot + p
        p_scr[rows, m * LANE:(m + 1) * LANE] = p.astype(BF16)
        used.add(m)
    for m in range(n_win_blocks):
        if m not in used:
            p_scr[rows, m * LANE:(m + 1) * LANE] = jnp.zeros((ATTN_STRIP, LANE), BF16)
    l_scr[rows, :] = jnp.broadcast_to(jnp.sum(tot, axis=-1, keepdims=True), (ATTN_STRIP, LANE))


def _attn_kernel(q_ref, k_ref, v_ref, kc_ref, vc_ref, bias_ref, o_ref, *scratch, n_rows):
    n_units = n_rows // ATTN_ROWS
    n_q = ATTN_ROWS * GRID_W
    n_win = ATTN_KEY_ROWS * GRID_W
    n_ctx = kc_ref.shape[0]
    n_win_blocks = n_win // LANE
    n_ctx_blocks = n_ctx // LANE
    variants = _attn_variants(n_rows)
    lane_q = lax.broadcasted_iota(jnp.int32, (n_q, LANE), 1)
    head_scale = _head_scales()
    contract_last = (((1,), (1,)), ((), ()))
    s_bufs, p_bufs, l_bufs = scratch[0::3], scratch[1::3], scratch[2::3]

    def key_start(g):
        return min(max(g * ATTN_ROWS - WIN_H // 2, 0), n_rows - ATTN_KEY_ROWS) * GRID_W

    def scores(g, slot):
        s_scr = s_bufs[slot]
        q = q_ref[pl.ds(g * n_q, n_q), :]
        kw = k_ref[pl.ds(key_start(g), n_win), :]
        for hh in range(2):
            qh = q * head_scale[hh]
            hr = slice(hh * n_q, (hh + 1) * n_q)
            s_scr[hr, 0:n_win] = lax.dot_general(qh, kw, contract_last, preferred_element_type=F32)
            s_scr[hr, n_win:n_win + n_ctx] = lax.dot_general(qh, kc_ref[...], contract_last,
                                                             preferred_element_type=F32)

    def softmax(kind, slot):
        geo = variants[kind]
        n_parts = GRID_W // ATTN_STRIP
        for hh in range(2):
            for i in range(ATTN_ROWS):
                jlo = geo["jlo"][i]
                blocks = []
                for m in range(n_win_blocks):
                    jl, jr = 2 * m, 2 * m + 1
                    vl = jlo <= jl < jlo + WIN_H
                    vr = jlo <= jr < jlo + WIN_H
                    if not (vl or vr):
                        continue
                    d = (jl - i + geo["droff"]) + WIN_H
                    blocks.append((m, d, None if (vl and vr) else ("lo" if vl else "hi")))
                for part in range(n_parts):
                    row0 = hh * n_q + i * GRID_W + part * ATTN_STRIP
                    _softmax_strip(s_bufs[slot], p_bufs[slot], l_bufs[slot], bias_ref, hh, row0, blocks,
                                   n_ctx_blocks, n_win_blocks)

    def pv(g, slot):
        p_scr = p_bufs[slot]
        vw = v_ref[pl.ds(key_start(g), n_win), :]
        o2 = jnp.dot(p_scr[:, 0:n_win], vw, preferred_element_type=F32)
        o2 = o2 + jnp.dot(p_scr[:, n_win:n_win + n_ctx], vc_ref[...], preferred_element_type=F32)
        o2 = o2 * (1.0 / l_bufs[slot][...])
        out = jnp.where(lane_q < GRID_W, o2[0:n_q], o2[n_q:2 * n_q])
        o_ref[pl.ds(g * n_q, n_q), :] = out.astype(o_ref.dtype)

    def kind(g):
        return "top" if g == 0 else ("bot" if g == n_units - 1 else "mid")

    for t in range(-1, n_units + 1):
        if t + 1 < n_units:
            scores(t + 1, (t + 1) % ATTN_SLOTS)
        if 0 <= t < n_units:
            softmax(kind(t), t % ATTN_SLOTS)
        if 0 <= t - 1:
            pv(t - 1, (t - 1) % ATTN_SLOTS)


def _attention(u, u_ctx, bias):
    b, l, _ = u.shape
    n_ctx = u_ctx.shape[1]
    n_rows = l // GRID_W
    n_pairs = NA_HEADS // 2
    n_q = ATTN_ROWS * GRID_W
    n_keys = ATTN_KEY_ROWS * GRID_W + n_ctx
    qb, kb, vb = COL_Q // LANE, COL_K // LANE, COL_V // LANE
    return pl.pallas_call(
        functools.partial(_attn_kernel, n_rows=n_rows),
        out_shape=jax.ShapeDtypeStruct((b, l, NA_DIM), BF16),
        grid=(b, n_pairs),
        in_specs=[
            pl.BlockSpec((None, l, LANE), lambda i, p: (i, 0, qb + p)),
            pl.BlockSpec((None, l, LANE), lambda i, p: (i, 0, kb + p)),
            pl.BlockSpec((None, l, LANE), lambda i, p: (i, 0, vb + p)),
            pl.BlockSpec((None, n_ctx, LANE), lambda i, p: (i, 0, kb + p)),
            pl.BlockSpec((None, n_ctx, LANE), lambda i, p: (i, 0, vb + p)),
            pl.BlockSpec((2, 2 * WIN_H, GRID_W, LANE), lambda i, p: (p, 0, 0, 0)),
        ],
        out_specs=pl.BlockSpec((None, l, LANE), lambda i, p: (i, 0, p)),
        scratch_shapes=ATTN_SLOTS * [
            pltpu.VMEM((2 * n_q, n_keys), F32),
            pltpu.VMEM((2 * n_q, n_keys), BF16),
            pltpu.VMEM((2 * n_q, LANE), F32),
        ],
        compiler_params=_params("arbitrary", "arbitrary"),
        name="natten",
    )(u, u, u, u_ctx, u_ctx, bias)


def _ctx_attn_pair(q, k, v):
    n_q = q.shape[0]
    lane_q = lax.broadcasted_iota(jnp.int32, (n_q, LANE), 1)
    head_scale = _head_scales()
    q2 = jnp.concatenate([q * head_scale[0], q * head_scale[1]], axis=0)
    s = lax.dot_general(q2, k, (((1,), (1,)), ((), ())), preferred_element_type=F32)
    p = jnp.exp(s - jnp.max(s, axis=-1, keepdims=True))
    rinv = 1.0 / jnp.sum(p, axis=-1, keepdims=True)
    o2 = jnp.dot(p.astype(BF16), v, preferred_element_type=F32) * rinv
    return jnp.where(lane_q < GRID_W, o2[0:n_q], o2[n_q:2 * n_q])


def _ctx_boundary_kernel(u_ref, yss_ref, x_ref, gate_ref, gpost_ref, cw_ref, w_out_ref,
                         shift_ref, scale_ref, g_ref, w_ref, wdt_ref,
                         x_out_ref, u_out_ref, dt_ref, dt_scr, yna_scr):
    for p in range(NA_HEADS // 2):
        lanes = slice(p * LANE, (p + 1) * LANE)
        q = u_ref[:, COL_Q + p * LANE:COL_Q + (p + 1) * LANE]
        k = u_ref[:, COL_K + p * LANE:COL_K + (p + 1) * LANE]
        v = u_ref[:, COL_V + p * LANE:COL_V + (p + 1) * LANE]
        yna_scr[:, lanes] = _ctx_attn_pair(q, k, v).astype(yna_scr.dtype)
    conv_cols = u_ref.at[:, 0:4 * CONV_DIM]
    halo = u_ref.at[0:16, 0:4 * CONV_DIM]
    x_new = _outproj_tile(0, 1, conv_cols, halo, halo, yna_scr, u_ref.at[:, COL_ZNA:COL_ZNA + NA_DIM], yss_ref,
                          x_ref, gate_ref, gpost_ref, cw_ref, w_out_ref)
    x_out_ref[...] = x_new
    _project_tile(_prenorm_tile(x_new, shift_ref, scale_ref, g_ref), w_ref, wdt_ref, u_out_ref, dt_ref, dt_scr)


def _ctx_boundary(u_ctx, y_ss, x_ctx, gate, g_post, conv_w, w_out, shift, scale, g_pre, w_u, w_dt):
    b, n, d = x_ctx.shape
    per_batch = lambda *shape: pl.BlockSpec((None,) + shape, lambda i: (i,) + (0,) * len(shape))
    const = lambda *shape: pl.BlockSpec(shape, lambda i: (0,) * len(shape))
    return pl.pallas_call(
        _ctx_boundary_kernel,
        out_shape=(jax.ShapeDtypeStruct((b, n, d), F32), jax.ShapeDtypeStruct((b, n, U_DIM), BF16),
                   jax.ShapeDtypeStruct((b, 2 * SSD_HEADS, n), F32)),
        grid=(b,),
        in_specs=[per_batch(n, U_DIM), per_batch(n, SSD_DIM), per_batch(n, d), per_batch(1, d), const(1, d),
                  const(3, CONV_DIM), const(d, d), per_batch(1, d), per_batch(1, d), const(1, d),
                  const(d, U_DIM), const(d, DT_PAD)],
        out_specs=(per_batch(n, d), per_batch(n, U_DIM), per_batch(2 * SSD_HEADS, n)),
        scratch_shapes=[pltpu.VMEM((n, DT_PAD), F32), pltpu.VMEM((n, NA_DIM), BF16)],
        compiler_params=_params("arbitrary"),
        name="ctx_boundary",
    )(u_ctx, y_ss, x_ctx, gate, g_post, conv_w, w_out, shift, scale, g_pre, w_u, w_dt)


def _attn_bias_table(rpb):
    heads = rpb.shape[0]
    cols = np.arange(GRID_W)
    c0 = np.clip(cols - WIN_W // 2, 0, GRID_W - WIN_W)
    rel = cols[None, :] - cols[:, None] + (WIN_W - 1)
    inside = (cols[None, :] >= c0[:, None]) & (cols[None, :] < c0[:, None] + WIN_W)
    select = ((rel[None] == np.arange(2 * WIN_W - 1)[:, None, None]) & inside[None]).astype(np.float32)
    blk = jnp.einsum("hrj,jck->hrck", rpb.astype(F32), jnp.asarray(select), precision=HIGHEST)
    blk = jnp.where(jnp.asarray(inside)[None, None], blk, NEG_INF)
    pad = jnp.full((heads, 1, GRID_W, GRID_W), NEG_INF, F32)
    left = jnp.concatenate([pad, blk], axis=1)
    right = jnp.concatenate([blk, pad], axis=1)
    return jnp.concatenate([left, right], axis=-1).astype(F32)


def _ssd_kernel(xl_ref, zl_ref, dtl_ref, xc_ref, zc_ref, dtc_ref, cw_ref, cb_ref, dtb_ref, alog_ref,
                dvec_ref, nw_ref, cos_ref, sin_ref, yl_ref, yc_ref,
                xs_l, bc_l, xs_c, bc_c, yacc_l, yacc_c):
    q = SSD_CHUNK
    ri = lax.broadcasted_iota(jnp.int32, (q, q), 0)
    ci = lax.broadcasted_iota(jnp.int32, (q, q), 1)
    lane = lax.broadcasted_iota(jnp.int32, (q, LANE), 1)
    lane1 = lax.broadcasted_iota(jnp.int32, (1, LANE), 1)
    rows8 = lax.broadcasted_iota(jnp.int32, (8, 1), 0)
    lo_half = lane < SSD_HEAD_DIM
    lo_half1 = lane1 < SSD_HEAD_DIM
    a_all = -jnp.exp(alog_ref[...])
    n_hd = 2 * SSD_HEADS

    def prep(raw_ref, c, n_chunks, rope, xs_ref, bc_ref):
        base = pl.multiple_of(c * q, q)
        seq = n_chunks * q
        x = raw_ref[pl.ds(base, q), :].astype(F32)
        pstart = pl.multiple_of(jnp.maximum(base - 16, 0), 16)
        nstart = pl.multiple_of(jnp.minimum(base + q, seq - 16), 16)
        prev = raw_ref[pl.ds(pstart, 16), :][15:16, :].astype(F32) * jnp.where(c > 0, 1.0, 0.0)
        nxt = raw_ref[pl.ds(nstart, 16), :][0:1, :].astype(F32) * jnp.where(c < n_chunks - 1, 1.0, 0.0)
        xm1 = pltpu.roll(x, 1, 0)
        xm1 = jnp.concatenate([jnp.where(rows8 == 0, prev, xm1[0:8]), xm1[8:]], axis=0)
        xp1 = pltpu.roll(x, q - 1, 0)
        xp1 = jnp.concatenate([xp1[:q - 8], jnp.where(rows8 == 7, nxt, xp1[q - 8:])], axis=0)
        y = xm1 * cw_ref[0:1, :] + x * cw_ref[1:2, :] + xp1 * cw_ref[2:3, :] + cb_ref[...]
        y = _silu(y)
        xs_ref[pl.ds(base, q), :] = y[:, 0:SSD_DIM]
        for t in range(2 * SSD_GROUPS):
            blk = y[:, SSD_DIM + t * SSD_STATE:SSD_DIM + (t + 1) * SSD_STATE]
            if rope:
                sw = jnp.where(jnp.bitwise_and(lane, 63) < 32, pltpu.roll(blk, 96, 1), pltpu.roll(blk, 32, 1))
                blk = blk * cos_ref[pl.ds(base, q), :] + sw * sin_ref[pl.ds(base, q), :]
            bc_ref[pl.ds(base, q), t * SSD_STATE:(t + 1) * SSD_STATE] = blk.astype(BF16)

    def scan_chunk(c, direction, dt_ref, xs_ref, bc_ref, states):
        base = pl.multiple_of(c * q, q)
        dt_t = _softplus(dt_ref[:, pl.ds(base, q)] + dtb_ref[...])
        tri = jnp.where(ri <= ci, 1.0, 0.0) if direction == 0 else jnp.where(ri >= ci, 1.0, 0.0)
        acs_t = jnp.dot(dt_t * a_all, tri.astype(F32), preferred_element_type=F32, precision=HIGHEST)
        stacked = jnp.concatenate([dt_t, acs_t, jnp.zeros((q - 2 * n_hd, q), F32)], axis=0)
        cols_all = stacked.T
        mask = (ri >= ci) if direction == 0 else (ri <= ci)
        end = q - 1 if direction == 0 else 0
        ys = []
        new_states = []
        for g in range(SSD_GROUPS):
            xg = xs_ref[pl.ds(base, q), g * LANE:(g + 1) * LANE]
            bg = bc_ref[pl.ds(base, q), g * SSD_STATE:(g + 1) * SSD_STATE]
            cg = bc_ref[pl.ds(base, q), (SSD_GROUPS + g) * SSD_STATE:(SSD_GROUPS + g + 1) * SSD_STATE]
            gram = lax.dot_general(cg, bg, (((1,), (1,)), ((), ())), preferred_element_type=F32)
            heads = [direction * SSD_HEADS + 2 * g, direction * SSD_HEADS + 2 * g + 1]
            a_bc = [jnp.broadcast_to(cols_all[:, n_hd + k:n_hd + k + 1], (q, LANE)) for k in heads]
            d_bc = [jnp.broadcast_to(cols_all[:, k:k + 1], (q, LANE)) for k in heads]
            a_row = [acs_t[k:k + 1, :] for k in heads]
            a_end = [acs_t[k:k + 1, end:end + 1] for k in heads]
            dtp = jnp.where(lo_half, d_bc[0], d_bc[1])
            acp = jnp.where(lo_half, a_bc[0], a_bc[1])
            a_end_p = jnp.where(lo_half1, a_end[0], a_end[1])
            xdt = (xg * dtp).astype(BF16)
            yd = []
            for k in range(2):
                decay = jnp.exp(jnp.where(mask, a_bc[k] - a_row[k], NEG_INF))
                yd.append(jnp.dot((gram * decay).astype(BF16), xdt, preferred_element_type=F32))
            y_diag = jnp.where(lo_half, yd[0], yd[1])
            st = states[g]
            y_off = jnp.dot(cg, st.astype(BF16), preferred_element_type=F32) * jnp.exp(acp)
            xw = (xg * (jnp.exp(a_end_p - acp) * dtp)).astype(BF16)
            bgt = bg.astype(F32).T.astype(BF16)
            upd = jnp.dot(bgt, xw, preferred_element_type=F32)
            new_states.append(st * jnp.exp(a_end_p) + upd)
            ys.append(y_diag + y_off)
        return jnp.concatenate(ys, axis=-1), tuple(new_states)

    def finish(y, c, xs_ref, z_ref, out_ref):
        base = pl.multiple_of(c * q, q)
        y = y + dvec_ref[...] * xs_ref[pl.ds(base, q), :]
        y = y * _silu(z_ref[pl.ds(base, q), :].astype(F32))
        out_ref[pl.ds(base, q), :] = (_rms(y) * nw_ref[...]).astype(out_ref.dtype)

    n_c = xc_ref.shape[0] // q
    n_l = xl_ref.shape[0] // q

    def fwd_pass(raw_ref, dt_ref, n_chunks, rope, xs_ref, bc_ref, yacc_ref, states):
        def body(c, states):
            prep(raw_ref, c, n_chunks, rope, xs_ref, bc_ref)
            base = pl.multiple_of(c * q, q)
            y, states = scan_chunk(c, 0, dt_ref, xs_ref, bc_ref, states)
            yacc_ref[pl.ds(base, q), :] = y
            return states

        return lax.fori_loop(0, n_chunks, body, states, unroll=min(n_chunks, SSD_UNROLL_FWD))

    def bwd_pass(dt_ref, n_chunks, xs_ref, bc_ref, yacc_ref, z_ref, out_ref, states):
        def body(t, states):
            c = n_chunks - 1 - t
            base = pl.multiple_of(c * q, q)
            y, states = scan_chunk(c, 1, dt_ref, xs_ref, bc_ref, states)
            finish(y + yacc_ref[pl.ds(base, q), :], c, xs_ref, z_ref, out_ref)
            return states

        return lax.fori_loop(0, n_chunks, body, states, unroll=min(n_chunks, SSD_UNROLL_BWD))

    zero_states = tuple(jnp.zeros((SSD_STATE, LANE), F32) for _ in range(SSD_GROUPS))
    states = fwd_pass(xc_ref, dtc_ref, n_c, False, xs_c, bc_c, yacc_c, zero_states)
    fwd_pass(xl_ref, dtl_ref, n_l, True, xs_l, bc_l, yacc_l, states)
    states = bwd_pass(dtc_ref, n_c, xs_c, bc_c, yacc_c, zc_ref, yc_ref, zero_states)
    bwd_pass(dtl_ref, n_l, xs_l, bc_l, yacc_l, zl_ref, yl_ref, states)


def _ssd(u, dt, u_ctx, dt_ctx, conv_w, conv_b, dt_bias, a_log, d_vec, norm_w, cos_t, sin_t):
    b, l, _ = u.shape
    n_ctx = u_ctx.shape[1]
    xb = COL_XBC // XBC_DIM
    zb = COL_ZSS // SSD_DIM
    n_bc = 2 * SSD_GROUPS * SSD_STATE
    const = lambda shape: pl.BlockSpec(shape, lambda i: (0,) * len(shape))
    return pl.pallas_call(
        _ssd_kernel,
        out_shape=(jax.ShapeDtypeStruct((b, l, SSD_DIM), BF16), jax.ShapeDtypeStruct((b, n_ctx, SSD_DIM), BF16)),
        grid=(b,),
        in_specs=[
            pl.BlockSpec((None, l, XBC_DIM), lambda i: (i, 0, xb)),
            pl.BlockSpec((None, l, SSD_DIM), lambda i: (i, 0, zb)),
            pl.BlockSpec((None, 2 * SSD_HEADS, l), lambda i: (i, 0, 0)),
            pl.BlockSpec((None, n_ctx, XBC_DIM), lambda i: (i, 0, xb)),
            pl.BlockSpec((None, n_ctx, SSD_DIM), lambda i: (i, 0, zb)),
            pl.BlockSpec((None, 2 * SSD_HEADS, n_ctx), lambda i: (i, 0, 0)),
            const((3, XBC_DIM)),
            const((1, XBC_DIM)),
            const((2 * SSD_HEADS, SSD_CHUNK)),
            const((2 * SSD_HEADS, SSD_CHUNK)),
            const((1, SSD_DIM)),
            const((1, SSD_DIM)),
            const((l, SSD_STATE)),
            const((l, SSD_STATE)),
        ],
        out_specs=(
            pl.BlockSpec((None, l, SSD_DIM), lambda i: (i, 0, 0)),
            pl.BlockSpec((None, n_ctx, SSD_DIM), lambda i: (i, 0, 0)),
        ),
        scratch_shapes=[
            pltpu.VMEM((l, SSD_DIM), F32),
            pltpu.VMEM((l, n_bc), BF16),
            pltpu.VMEM((n_ctx, SSD_DIM), F32),
            pltpu.VMEM((n_ctx, n_bc), BF16),
            pltpu.VMEM((l, SSD_DIM), F32),
            pltpu.VMEM((n_ctx, SSD_DIM), F32),
        ],
        compiler_params=_params("arbitrary"),
        name="ssd",
    )(u, u, dt, u_ctx, u_ctx, dt_ctx, conv_w, conv_b, dt_bias, a_log, d_vec, norm_w, cos_t, sin_t)


def _rope_tables(seq):
    n_freq = SSD_STATE // 4
    pos = jnp.arange(seq)
    row_pos = (pos // GRID_W).astype(F32)
    col_pos = (pos % GRID_W).astype(F32)
    inv_freq = ROPE_BASE ** (-jnp.arange(n_freq, dtype=F32) / n_freq)
    ar = row_pos[:, None] * inv_freq
    ac = col_pos[:, None] * inv_freq
    cos_t = jnp.concatenate([jnp.cos(ar), jnp.cos(ar), jnp.cos(ac), jnp.cos(ac)], axis=-1)
    sin_t = jnp.concatenate([-jnp.sin(ar), jnp.sin(ar), -jnp.sin(ac), jnp.sin(ac)], axis=-1)
    return cos_t, sin_t


def _outproj_tile(j, nt, ua_ref, up_ref, un_ref, yna_ref, zna_ref, yss_ref, x_ref, gate_ref, gpost_ref, cw_ref,
                  w_ref):
    tm = ua_ref.shape[0]
    cd = CONV_DIM
    rows8 = lax.broadcasted_iota(jnp.int32, (8, 1), 0)
    ua = ua_ref[...]
    t = ua[:, 2 * cd:3 * cd].astype(F32) * ua[:, 0:cd].astype(F32)
    up = up_ref[15:16, :]
    un = un_ref[0:1, :]
    tprev = up[:, 2 * cd:3 * cd].astype(F32) * up[:, 0:cd].astype(F32) * jnp.where(j > 0, 1.0, 0.0)
    tnext = un[:, 2 * cd:3 * cd].astype(F32) * un[:, 0:cd].astype(F32) * jnp.where(j < nt - 1, 1.0, 0.0)
    tm1 = pltpu.roll(t, 1, 0)
    tm1 = jnp.concatenate([jnp.where(rows8 == 0, tprev, tm1[0:8]), tm1[8:]], axis=0)
    tp1 = pltpu.roll(t, tm - 1, 0)
    tp1 = jnp.concatenate([tp1[:tm - 8], jnp.where(rows8 == 7, tnext, tp1[tm - 8:])], axis=0)
    conv = tm1 * cw_ref[0:1, :] + t * cw_ref[1:2, :] + tp1 * cw_ref[2:3, :]
    ysc = ua[:, cd:2 * cd].astype(F32) * conv * _silu(ua[:, 3 * cd:4 * cd].astype(F32))
    yna = yna_ref[...].astype(F32) * _silu(zna_ref[...].astype(F32))
    ycat = jnp.concatenate([ysc.astype(BF16), yna.astype(BF16), yss_ref[...]], axis=-1)
    out = jnp.dot(ycat, w_ref[...], preferred_element_type=F32)
    return x_ref[...] + gate_ref[...] * (_rms(out) * gpost_ref[...])


def _outproj_kernel(*refs):
    refs[-1][...] = _outproj_tile(pl.program_id(1), pl.num_programs(1), *refs[:-1])


N_OUTPROJ_IN = 11


def _layer_boundary_kernel(*refs):
    out_in = refs[:N_OUTPROJ_IN]
    shift_ref, scale_ref, g_ref, w_ref, wdt_ref = refs[N_OUTPROJ_IN:N_OUTPROJ_IN + 5]
    x_out_ref, u_ref, dt_ref, dt_scr, x_keep = refs[N_OUTPROJ_IN + 5:]
    j = pl.program_id(1)
    nt = pl.num_programs(1) - 1

    def in_half():
        hb = _prenorm_tile(x_keep[...], shift_ref, scale_ref, g_ref)
        _project_tile(hb, w_ref, wdt_ref, u_ref, dt_ref, dt_scr)

    def out_half():
        x_new = _outproj_tile(j, nt, *out_in)
        x_out_ref[...] = x_new
        x_keep[...] = x_new

    @pl.when(j == 0)
    def _():
        out_half()

    @pl.when(jnp.logical_and(j > 0, j < nt))
    def _():
        in_half()
        out_half()

    @pl.when(j == nt)
    def _():
        in_half()


def _outproj_specs(l, d, tm):
    hb = tm // 16
    n_hb = l // 16
    zb = COL_ZNA // NA_DIM
    last = l // tm - 1
    tile = lambda j: jnp.minimum(j, last)
    return [
        pl.BlockSpec((None, tm, 4 * CONV_DIM), lambda i, j: (i, tile(j), 0)),
        pl.BlockSpec((None, 16, 4 * CONV_DIM), lambda i, j: (i, jnp.maximum(tile(j) * hb - 1, 0), 0)),
        pl.BlockSpec((None, 16, 4 * CONV_DIM), lambda i, j: (i, jnp.minimum((tile(j) + 1) * hb, n_hb - 1), 0)),
        pl.BlockSpec((None, tm, NA_DIM), lambda i, j: (i, tile(j), 0)),
        pl.BlockSpec((None, tm, NA_DIM), lambda i, j: (i, tile(j), zb)),
        pl.BlockSpec((None, tm, SSD_DIM), lambda i, j: (i, tile(j), 0)),
        pl.BlockSpec((None, tm, d), lambda i, j: (i, tile(j), 0)),
        pl.BlockSpec((None, 1, d), lambda i, j: (i, 0, 0)),
        pl.BlockSpec((1, d), lambda i, j: (0, 0)),
        pl.BlockSpec((3, CONV_DIM), lambda i, j: (0, 0)),
        pl.BlockSpec((d, d), lambda i, j: (0, 0)),
    ]


def _outproj(u, y_na, y_ss, x, gate, g_post, conv_w, w_out, tm):
    b, l, d = x.shape
    return pl.pallas_call(
        _outproj_kernel,
        out_shape=jax.ShapeDtypeStruct((b, l, d), F32),
        grid=(b, l // tm),
        in_specs=_outproj_specs(l, d, tm),
        out_specs=pl.BlockSpec((None, tm, d), lambda i, j: (i, j, 0)),
        compiler_params=_params("arbitrary", "arbitrary"),
        name="outproj",
    )(u, u, u, y_na, u, y_ss, x, gate, g_post, conv_w, w_out)


def _layer_boundary(u, y_na, y_ss, x, gate, g_post, conv_w, w_out, shift, scale, g_pre, w_u, w_dt, tm):
    b, l, d = x.shape
    last = l // tm - 1
    return pl.pallas_call(
        _layer_boundary_kernel,
        out_shape=(jax.ShapeDtypeStruct((b, l, d), F32), jax.ShapeDtypeStruct((b, l, U_DIM), BF16),
                   jax.ShapeDtypeStruct((b, 2 * SSD_HEADS, l), F32)),
        grid=(b, l // tm + 1),
        in_specs=_outproj_specs(l, d, tm) + [
            pl.BlockSpec((None, 1, d), lambda i, j: (i, 0, 0)),
            pl.BlockSpec((None, 1, d), lambda i, j: (i, 0, 0)),
            pl.BlockSpec((1, d), lambda i, j: (0, 0)),
            pl.BlockSpec((d, U_DIM), lambda i, j: (0, 0)),
            pl.BlockSpec((d, DT_PAD), lambda i, j: (0, 0)),
        ],
        out_specs=(
            pl.BlockSpec((None, tm, d), lambda i, j: (i, jnp.minimum(j, last), 0)),
            pl.BlockSpec((None, tm, U_DIM), lambda i, j: (i, jnp.maximum(j - 1, 0), 0)),
            pl.BlockSpec((None, 2 * SSD_HEADS, tm), lambda i, j: (i, 0, jnp.maximum(j - 1, 0))),
        ),
        scratch_shapes=[pltpu.VMEM((tm, DT_PAD), F32), pltpu.VMEM((tm, d), F32)],
        compiler_params=_params("arbitrary", "arbitrary"),
        name="layer_boundary",
    )(u, u, u, y_na, u, y_ss, x, gate, g_post, conv_w, w_out, shift, scale, g_pre, w_u, w_dt)


def kernel(x, c, ctx, c_ctx, w_ada, b_ada, g_pre, g_post, w_in, conv_a_w, rpb, ssd_conv_w, ssd_conv_b,
           dt_bias_f, dt_bias_b, a_log_f, a_log_b, ssd_d, ssd_norm_w, w_out):
    depth = w_ada.shape[0]
    bsz, seq, d = x.shape
    n_ctx = ctx.shape[1]
    assert d == D_MODEL and seq % (2 * ATTN_ROWS * GRID_W) == 0 and seq // GRID_W >= ATTN_KEY_ROWS
    assert n_ctx % SSD_CHUNK == 0 and w_in.shape[-1] == U_DIM + 2 * SSD_HEADS
    tm_lat = min(512, seq)
    tm_ctx = n_ctx

    n_rows = -(-(bsz + 1) // 8) * 8
    cc = jnp.zeros((n_rows, d), F32).at[:bsz].set(c).at[bsz].set(c_ctx)
    mods = _ada_all_layers(cc, w_ada, b_ada)
    cos_t, sin_t = _rope_tables(seq)
    pad8 = DT_PAD - 2 * SSD_HEADS

    def layer_inputs(layer):
        m = mods[layer]
        lat = tuple(m[:bsz, k * d:(k + 1) * d].reshape(bsz, 1, d) for k in range(3))
        ctx_mod = tuple(jnp.broadcast_to(m[bsz, k * d:(k + 1) * d], (bsz, 1, d)) for k in range(3))
        w_u = w_in[layer, :, :U_DIM].astype(BF16)
        w_dt = jnp.pad(w_in[layer, :, U_DIM:], ((0, 0), (0, pad8))).astype(BF16)
        return lat, ctx_mod, w_u, w_dt, g_pre[layer].reshape(1, d)

    x_ctx = ctx
    (shift, scale, gate), (shift_c, scale_c, gate_c), w_u, w_dt, gpre = layer_inputs(0)
    u_lat, dt_lat = _inproj(x, shift, scale, gpre, w_u, w_dt, tm_lat)
    u_ctx, dt_ctx = _inproj(x_ctx, shift_c, scale_c, gpre, w_u, w_dt, tm_ctx)
    for layer in range(depth):
        gpost = g_post[layer].reshape(1, d)
        per_head = lambda f, bwd: jnp.broadcast_to(jnp.concatenate([f, bwd])[:, None], (2 * SSD_HEADS, SSD_CHUNK))
        dt_bias = per_head(dt_bias_f[layer], dt_bias_b[layer])
        a_log = per_head(a_log_f[layer], a_log_b[layer])
        d_vec = jnp.repeat(ssd_d[layer], SSD_HEAD_DIM).reshape(1, SSD_DIM)
        bias = _attn_bias_table(rpb[layer])

        y_na = _attention(u_lat, u_ctx, bias)
        y_ss, y_ss_ctx = _ssd(u_lat, dt_lat, u_ctx, dt_ctx, ssd_conv_w[layer], ssd_conv_b[layer].reshape(1, -1),
                              dt_bias, a_log, d_vec, ssd_norm_w[layer].reshape(1, -1), cos_t, sin_t)
        w_o = w_out[layer].astype(BF16)
        if layer == depth - 1:
            return _outproj(u_lat, y_na, y_ss, x, gate, gpost, conv_a_w[layer], w_o, tm_lat)
        (shift, scale, gate_next), (shift_c, scale_c, gate_c_next), w_u, w_dt, gpre = layer_inputs(layer + 1)
        x, u_lat, dt_lat = _layer_boundary(u_lat, y_na, y_ss, x, gate, gpost, conv_a_w[layer], w_o,
                                           shift, scale, gpre, w_u, w_dt, tm_lat)
        x_ctx, u_ctx, dt_ctx = _ctx_boundary(u_ctx, y_ss_ctx, x_ctx, gate_c, gpost, conv_a_w[layer], w_o,
                                             shift_c, scale_c, gpre, w_u, w_dt)
        gate, gate_c = gate_next, gate_c_next
```

```python
import functools
import math

import jax
import jax.numpy as jnp
import numpy as np
from jax import lax
from jax.experimental import pallas as pl
from jax.experimental.pallas import tpu as pltpu

F32 = jnp.float32
BF16 = jnp.bfloat16
HIGHEST = lax.Precision.HIGHEST

D_MODEL = 1024
GRID_W = 64
EPS = 1e-6
CONV_DIM = 256
NA_HEADS = 8
NA_HEAD_DIM = 64
NA_DIM = NA_HEADS * NA_HEAD_DIM
WIN_H = 8
WIN_W = 16
SSD_HEADS = 4
SSD_HEAD_DIM = 64
SSD_DIM = SSD_HEADS * SSD_HEAD_DIM
SSD_GROUPS = 2
SSD_STATE = 128
SSD_CHUNK = 128
XBC_DIM = SSD_DIM + 2 * SSD_GROUPS * SSD_STATE
ROPE_BASE = 10000.0
U_DIM = 4 * CONV_DIM + 4 * NA_DIM + XBC_DIM + SSD_DIM
DT_PAD = 128
COL_Q = 4 * CONV_DIM
COL_K = COL_Q + NA_DIM
COL_V = COL_K + NA_DIM
COL_ZNA = COL_V + NA_DIM
COL_XBC = COL_ZNA + NA_DIM
COL_ZSS = COL_XBC + XBC_DIM

LANE = 128
ATTN_ROWS = 4
ATTN_KEY_ROWS = 12
ATTN_STRIP = 16
ATTN_SLOTS = 3
ATTN_PAIRS_PER_STEP = 1
SSD_UNROLL_FWD = 4
SSD_UNROLL_BWD = 8
VMEM_LIMIT = 56 * 1024 * 1024

NEG_INF = float("-inf")


def _sigmoid(x):
    return 1.0 / (1.0 + jnp.exp(-x))


def _silu(x):
    return x * _sigmoid(x)


def _softplus(x):
    return jnp.maximum(x, 0.0) + jnp.log1p(jnp.exp(-jnp.abs(x)))


def _rms(x):
    return x * lax.rsqrt(jnp.mean(x * x, axis=-1, keepdims=True) + EPS)


def _params(*sem):
    return pltpu.CompilerParams(dimension_semantics=sem, vmem_limit_bytes=VMEM_LIMIT)


def _ada_kernel(cc_ref, w_ref, b_ref, o_ref):
    s = _silu(cc_ref[...])
    o_ref[...] = jnp.dot(s, w_ref[...], preferred_element_type=F32, precision=HIGHEST) + b_ref[...]


def _ada_all_layers(cc, w_ada, b_ada):
    depth = w_ada.shape[0]
    r = cc.shape[0]
    nblk = 3 * D_MODEL // D_MODEL
    return pl.pallas_call(
        _ada_kernel,
        out_shape=jax.ShapeDtypeStruct((depth, r, 3 * D_MODEL), F32),
        grid=(depth, nblk),
        in_specs=[
            pl.BlockSpec((r, D_MODEL), lambda l, n: (0, 0)),
            pl.BlockSpec((None, D_MODEL, D_MODEL), lambda l, n: (l, 0, n)),
            pl.BlockSpec((None, 1, D_MODEL), lambda l, n: (l, 0, n)),
        ],
        out_specs=pl.BlockSpec((None, r, D_MODEL), lambda l, n: (l, 0, n)),
        compiler_params=_params("arbitrary", "arbitrary"),
        name="ada",
    )(cc, w_ada, b_ada.reshape(depth, 1, 3 * D_MODEL))


def _prenorm_tile(x, shift_ref, scale_ref, g_ref):
    h = _rms(x) * g_ref[...]
    h = h * (1.0 + scale_ref[...]) + shift_ref[...]
    return h.astype(BF16)


def _project_cols(hb, w_ref, u_ref, lo, hi):
    nb = 512
    for n in range(lo, hi, nb):
        u_ref[:, n:n + nb] = jnp.dot(hb, w_ref[:, n:n + nb], preferred_element_type=F32).astype(BF16)


def _project_dt(hb, wdt_ref, dt_ref, dt_scr):
    dt_scr[...] = jnp.dot(hb, wdt_ref[...], preferred_element_type=F32)
    dt_ref[...] = dt_scr[...].T[0:2 * SSD_HEADS, :]


def _project_tile(hb, w_ref, wdt_ref, u_ref, dt_ref, dt_scr):
    _project_cols(hb, w_ref, u_ref, 0, U_DIM)
    _project_dt(hb, wdt_ref, dt_ref, dt_scr)


def _inproj_kernel(x_ref, shift_ref, scale_ref, g_ref, w_ref, wdt_ref, u_ref, dt_ref, dt_scr):
    _project_tile(_prenorm_tile(x_ref[...], shift_ref, scale_ref, g_ref), w_ref, wdt_ref, u_ref, dt_ref, dt_scr)


def _inproj(x, shift, scale, g_pre, w_u, w_dt, tm):
    b, l, d = x.shape
    return pl.pallas_call(
        _inproj_kernel,
        out_shape=(jax.ShapeDtypeStruct((b, l, U_DIM), BF16), jax.ShapeDtypeStruct((b, 2 * SSD_HEADS, l), F32)),
        grid=(b, l // tm),
        in_specs=[
            pl.BlockSpec((None, tm, d), lambda i, j: (i, j, 0)),
            pl.BlockSpec((None, 1, d), lambda i, j: (i, 0, 0)),
            pl.BlockSpec((None, 1, d), lambda i, j: (i, 0, 0)),
            pl.BlockSpec((1, d), lambda i, j: (0, 0)),
            pl.BlockSpec((d, U_DIM), lambda i, j: (0, 0)),
            pl.BlockSpec((d, DT_PAD), lambda i, j: (0, 0)),
        ],
        out_specs=(
            pl.BlockSpec((None, tm, U_DIM), lambda i, j: (i, j, 0)),
            pl.BlockSpec((None, 2 * SSD_HEADS, tm), lambda i, j: (i, 0, j)),
        ),
        scratch_shapes=[pltpu.VMEM((tm, DT_PAD), F32)],
        compiler_params=_params("arbitrary", "arbitrary"),
        name="inproj",
    )(x, shift, scale, g_pre, w_u, w_dt)


def _attn_variants(n_rows):
    return {
        "top": dict(jlo=[0, 0, 0, 0], droff=0),
        "mid": dict(jlo=[0, 1, 2, 3], droff=-WIN_H // 2),
        "bot": dict(jlo=[4, 4, 4, 4], droff=-WIN_H),
    }


def _head_scales():
    lane1 = lax.broadcasted_iota(jnp.int32, (1, LANE), 1)
    scale = NA_HEAD_DIM ** -0.5
    lo = jnp.where(lane1 < NA_HEAD_DIM, scale, 0.0).astype(BF16)
    hi = jnp.where(lane1 >= NA_HEAD_DIM, scale, 0.0).astype(BF16)
    return lo, hi


def _softmax_strip(s_scr, p_scr, l_scr, bias_ref, hh, row0, blocks, n_ctx_blocks, n_win_blocks):
    rows = slice(row0, row0 + ATTN_STRIP)
    brow = (row0 % GRID_W)
    lane = lax.broadcasted_iota(jnp.int32, (ATTN_STRIP, LANE), 1)
    def biased(m, d, keep, shift):
        s = s_scr[rows, m * LANE:(m + 1) * LANE]
        if shift is not None:
            s = s - shift
        if d is not None:
            s = s + bias_ref[hh, d, brow:brow + ATTN_STRIP, :]
        if keep == "lo":
            s = jnp.where(lane < GRID_W, s, NEG_INF)
        elif keep == "hi":
            s = jnp.where(lane >= GRID_W, s, NEG_INF)
        return s

    all_blocks = list(blocks) + [(n_win_blocks + c, None, None) for c in range(n_ctx_blocks)]
    mx = None
    for m, d, keep in all_blocks:
        s = biased(m, d, keep, None)
        mx = s if mx is None else jnp.maximum(mx, s)
    mx = jnp.max(mx, axis=-1, keepdims=True)
    tot = None
    used = set()
    for m, d, keep in all_blocks:
        p = jnp.exp(biased(m, d, keep, mx))
        tot = p if tot is None else tot + p
        p_scr[rows, m * LANE:(m + 1) * LANE] = p.astype(BF16)
        used.add(m)
    for m in range(n_win_blocks):
        if m not in used:
            p_scr[rows, m * LANE:(m + 1) * LANE] = jnp.zeros((ATTN_STRIP, LANE), BF16)
    l_scr[rows, :] = jnp.broadcast_to(jnp.sum(tot, axis=-1, keepdims=True), (ATTN_STRIP, LANE))


def _attn_kernel(q_ref, k_ref, v_ref, kc_ref, vc_ref, bias_ref, o_ref, *scratch, n_rows):
    n_units = n_rows // ATTN_ROWS
    n_q = ATTN_ROWS * GRID_W
    n_win = ATTN_KEY_ROWS * GRID_W
    n_ctx = kc_ref.shape[0]
    n_win_blocks = n_win // LANE
    n_ctx_blocks = n_ctx // LANE
    variants = _attn_variants(n_rows)
    lane_q = lax.broadcasted_iota(jnp.int32, (n_q, LANE), 1)
    head_scale = _head_scales()
    contract_last = (((1,), (1,)), ((), ()))
    s_bufs, p_bufs, l_bufs = scratch[0::3], scratch[1::3], scratch[2::3]

    def lanes(pr):
        return slice(pr * LANE, (pr + 1) * LANE)

    def key_start(g):
        return min(max(g * ATTN_ROWS - WIN_H // 2, 0), n_rows - ATTN_KEY_ROWS) * GRID_W

    def scores(pr, g, slot):
        s_scr = s_bufs[slot]
        q = q_ref[pl.ds(g * n_q, n_q), lanes(pr)]
        kw = k_ref[pl.ds(key_start(g), n_win), lanes(pr)]
        for hh in range(2):
            qh = q * head_scale[hh]
            hr = slice(hh * n_q, (hh + 1) * n_q)
            s_scr[hr, 0:n_win] = lax.dot_general(qh, kw, contract_last, preferred_element_type=F32)
            s_scr[hr, n_win:n_win + n_ctx] = lax.dot_general(qh, kc_ref[:, lanes(pr)], contract_last,
                                                             preferred_element_type=F32)

    def softmax(pr, kind, slot):
        geo = variants[kind]
        n_parts = GRID_W // ATTN_STRIP
        for hh in range(2):
            for i in range(ATTN_ROWS):
                jlo = geo["jlo"][i]
                blocks = []
                for m in range(n_win_blocks):
                    jl, jr = 2 * m, 2 * m + 1
                    vl = jlo <= jl < jlo + WIN_H
                    vr = jlo <= jr < jlo + WIN_H
                    if not (vl or vr):
                        continue
                    d = (jl - i + geo["droff"]) + WIN_H
                    blocks.append((m, d, None if (vl and vr) else ("lo" if vl else "hi")))
                for part in range(n_parts):
                    row0 = hh * n_q + i * GRID_W + part * ATTN_STRIP
                    _softmax_strip(s_bufs[slot], p_bufs[slot], l_bufs[slot], bias_ref, 2 * pr + hh, row0, blocks,
                                   n_ctx_blocks, n_win_blocks)

    def pv(pr, g, slot):
        p_scr = p_bufs[slot]
        vw = v_ref[pl.ds(key_start(g), n_win), lanes(pr)]
        o2 = jnp.dot(p_scr[:, 0:n_win], vw, preferred_element_type=F32)
        o2 = o2 + jnp.dot(p_scr[:, n_win:n_win + n_ctx], vc_ref[:, lanes(pr)], preferred_element_type=F32)
        o2 = o2 * (1.0 / l_bufs[slot][...])
        out = jnp.where(lane_q < GRID_W, o2[0:n_q], o2[n_q:2 * n_q])
        o_ref[pl.ds(g * n_q, n_q), lanes(pr)] = out.astype(o_ref.dtype)

    def kind(g):
        return "top" if g == 0 else ("bot" if g == n_units - 1 else "mid")

    n_stream = ATTN_PAIRS_PER_STEP * n_units
    for t in range(-1, n_stream + 1):
        if t + 1 < n_stream:
            scores((t + 1) // n_units, (t + 1) % n_units, (t + 1) % ATTN_SLOTS)
        if 0 <= t < n_stream:
            softmax(t // n_units, kind(t % n_units), t % ATTN_SLOTS)
        if 0 <= t - 1:
            pv((t - 1) // n_units, (t - 1) % n_units, (t - 1) % ATTN_SLOTS)


def _attention(u, u_ctx, bias):
    b, l, _ = u.shape
    n_ctx = u_ctx.shape[1]
    n_rows = l // GRID_W
    pps = ATTN_PAIRS_PER_STEP
    width = pps * LANE
    n_q = ATTN_ROWS * GRID_W
    n_keys = ATTN_KEY_ROWS * GRID_W + n_ctx
    qb, kb, vb = COL_Q // width, COL_K // width, COL_V // width
    return pl.pallas_call(
        functools.partial(_attn_kernel, n_rows=n_rows),
        out_shape=jax.ShapeDtypeStruct((b, l, NA_DIM), BF16),
        grid=(b, NA_HEADS // (2 * pps)),
        in_specs=[
            pl.BlockSpec((None, l, width), lambda i, p: (i, 0, qb + p)),
            pl.BlockSpec((None, l, width), lambda i, p: (i, 0, kb + p)),
            pl.BlockSpec((None, l, width), lambda i, p: (i, 0, vb + p)),
            pl.BlockSpec((None, n_ctx, width), lambda i, p: (i, 0, kb + p)),
            pl.BlockSpec((None, n_ctx, width), lambda i, p: (i, 0, vb + p)),
            pl.BlockSpec((2 * pps, 2 * WIN_H, GRID_W, LANE), lambda i, p: (p, 0, 0, 0)),
        ],
        out_specs=pl.BlockSpec((None, l, width), lambda i, p: (i, 0, p)),
        scratch_shapes=ATTN_SLOTS * [
            pltpu.VMEM((2 * n_q, n_keys), F32),
            pltpu.VMEM((2 * n_q, n_keys), BF16),
            pltpu.VMEM((2 * n_q, LANE), F32),
        ],
        compiler_params=_params("arbitrary", "arbitrary"),
        name="natten",
    )(u, u, u, u_ctx, u_ctx, bias)


def _ctx_attn_pair(q, k, v):
    n_q = q.shape[0]
    lane_q = lax.broadcasted_iota(jnp.int32, (n_q, LANE), 1)
    head_scale = _head_scales()
    q2 = jnp.concatenate([q * head_scale[0], q * head_scale[1]], axis=0)
    s = lax.dot_general(q2, k, (((1,), (1,)), ((), ())), preferred_element_type=F32)
    p = jnp.exp(s - jnp.max(s, axis=-1, keepdims=True))
    rinv = 1.0 / jnp.sum(p, axis=-1, keepdims=True)
    o2 = jnp.dot(p.astype(BF16), v, preferred_element_type=F32) * rinv
    return jnp.where(lane_q < GRID_W, o2[0:n_q], o2[n_q:2 * n_q])


def _ctx_boundary_kernel(u_ref, yss_ref, x_ref, gate_ref, gpost_ref, cw_ref, w_out_ref,
                         shift_ref, scale_ref, g_ref, w_ref, wdt_ref,
                         x_out_ref, u_out_ref, dt_ref, dt_scr, yna_scr):
    for p in range(NA_HEADS // 2):
        lanes = slice(p * LANE, (p + 1) * LANE)
        q = u_ref[:, COL_Q + p * LANE:COL_Q + (p + 1) * LANE]
        k = u_ref[:, COL_K + p * LANE:COL_K + (p + 1) * LANE]
        v = u_ref[:, COL_V + p * LANE:COL_V + (p + 1) * LANE]
        yna_scr[:, lanes] = _ctx_attn_pair(q, k, v).astype(yna_scr.dtype)
    conv_cols = u_ref.at[:, 0:4 * CONV_DIM]
    halo = u_ref.at[0:16, 0:4 * CONV_DIM]
    x_new = _outproj_tile(0, 1, conv_cols, halo, halo, yna_scr, u_ref.at[:, COL_ZNA:COL_ZNA + NA_DIM], yss_ref,
                          x_ref, gate_ref, gpost_ref, cw_ref, w_out_ref)
    x_out_ref[...] = x_new
    _project_tile(_prenorm_tile(x_new, shift_ref, scale_ref, g_ref), w_ref, wdt_ref, u_out_ref, dt_ref, dt_scr)


def _ctx_boundary(u_ctx, y_ss, x_ctx, gate, g_post, conv_w, w_out, shift, scale, g_pre, w_u, w_dt):
    b, n, d = x_ctx.shape
    per_batch = lambda *shape: pl.BlockSpec((None,) + shape, lambda i: (i,) + (0,) * len(shape))
    const = lambda *shape: pl.BlockSpec(shape, lambda i: (0,) * len(shape))
    return pl.pallas_call(
        _ctx_boundary_kernel,
        out_shape=(jax.ShapeDtypeStruct((b, n, d), F32), jax.ShapeDtypeStruct((b, n, U_DIM), BF16),
                   jax.ShapeDtypeStruct((b, 2 * SSD_HEADS, n), F32)),
        grid=(b,),
        in_specs=[per_batch(n, U_DIM), per_batch(n, SSD_DIM), per_batch(n, d), per_batch(1, d), const(1, d),
                  const(3, CONV_DIM), const(d, d), per_batch(1, d), per_batch(1, d), const(1, d),
                  const(d, U_DIM), const(d, DT_PAD)],
        out_specs=(per_batch(n, d), per_batch(n, U_DIM), per_batch(2 * SSD_HEADS, n)),
        scratch_shapes=[pltpu.VMEM((n, DT_PAD), F32), pltpu.VMEM((n, NA_DIM), BF16)],
        compiler_params=_params("arbitrary"),
        name="ctx_boundary",
    )(u_ctx, y_ss, x_ctx, gate, g_post, conv_w, w_out, shift, scale, g_pre, w_u, w_dt)


def _attn_bias_table(rpb):
    heads = rpb.shape[0]
    cols = np.arange(GRID_W)
    c0 = np.clip(cols - WIN_W // 2, 0, GRID_W - WIN_W)
    rel = cols[None, :] - cols[:, None] + (WIN_W - 1)
    inside = (cols[None, :] >= c0[:, None]) & (cols[None, :] < c0[:, None] + WIN_W)
    select = ((rel[None] == np.arange(2 * WIN_W - 1)[:, None, None]) & inside[None]).astype(np.float32)
    blk = jnp.einsum("hrj,jck->hrck", rpb.astype(F32), jnp.asarray(select), precision=HIGHEST)
    blk = jnp.where(jnp.asarray(inside)[None, None], blk, NEG_INF)
    pad = jnp.full((heads, 1, GRID_W, GRID_W), NEG_INF, F32)
    left = jnp.concatenate([pad, blk], axis=1)
    right = jnp.concatenate([blk, pad], axis=1)
    return jnp.concatenate([left, right], axis=-1).astype(F32)


def _ssd_kernel(xl_ref, zl_ref, dtl_ref, xc_ref, zc_ref, dtc_ref, cw_ref, cb_ref, dtb_ref, alog_ref,
                dvec_ref, nw_ref, cos_ref, sin_ref, yl_ref, yc_ref,
                xs_l, bc_l, xs_c, bc_c, yacc_l, yacc_c):
    q = SSD_CHUNK
    ri = lax.broadcasted_iota(jnp.int32, (q, q), 0)
    ci = lax.broadcasted_iota(jnp.int32, (q, q), 1)
    lane = lax.broadcasted_iota(jnp.int32, (q, LANE), 1)
    lane1 = lax.broadcasted_iota(jnp.int32, (1, LANE), 1)
    rows8 = lax.broadcasted_iota(jnp.int32, (8, 1), 0)
    lo_half = lane < SSD_HEAD_DIM
    lo_half1 = lane1 < SSD_HEAD_DIM
    a_all = -jnp.exp(alog_ref[...])
    n_hd = 2 * SSD_HEADS

    def prep(raw_ref, c, n_chunks, rope, xs_ref, bc_ref):
        base = pl.multiple_of(c * q, q)
        seq = n_chunks * q
        x = raw_ref[pl.ds(base, q), :].astype(F32)
        pstart = pl.multiple_of(jnp.maximum(base - 16, 0), 16)
        nstart = pl.multiple_of(jnp.minimum(base + q, seq - 16), 16)
        prev = raw_ref[pl.ds(pstart, 16), :][15:16, :].astype(F32) * jnp.where(c > 0, 1.0, 0.0)
        nxt = raw_ref[pl.ds(nstart, 16), :][0:1, :].astype(F32) * jnp.where(c < n_chunks - 1, 1.0, 0.0)
        xm1 = pltpu.roll(x, 1, 0)
        xm1 = jnp.concatenate([jnp.where(rows8 == 0, prev, xm1[0:8]), xm1[8:]], axis=0)
        xp1 = pltpu.roll(x, q - 1, 0)
        xp1 = jnp.concatenate([xp1[:q - 8], jnp.where(rows8 == 7, nxt, xp1[q - 8:])], axis=0)
        y = xm1 * cw_ref[0:1, :] + x * cw_ref[1:2, :] + xp1 * cw_ref[2:3, :] + cb_ref[...]
        y = _silu(y)
        xs_ref[pl.ds(base, q), :] = y[:, 0:SSD_DIM]
        for t in range(2 * SSD_GROUPS):
            blk = y[:, SSD_DIM + t * SSD_STATE:SSD_DIM + (t + 1) * SSD_STATE]
            if rope:
                sw = jnp.where(jnp.bitwise_and(lane, 63) < 32, pltpu.roll(blk, 96, 1), pltpu.roll(blk, 32, 1))
                blk = blk * cos_ref[pl.ds(base, q), :] + sw * sin_ref[pl.ds(base, q), :]
            bc_ref[pl.ds(base, q), t * SSD_STATE:(t + 1) * SSD_STATE] = blk.astype(BF16)

    def scan_chunk(c, direction, dt_ref, xs_ref, bc_ref, states):
        base = pl.multiple_of(c * q, q)
        dt_t = _softplus(dt_ref[:, pl.ds(base, q)] + dtb_ref[...])
        tri = jnp.where(ri <= ci, 1.0, 0.0) if direction == 0 else jnp.where(ri >= ci, 1.0, 0.0)
        acs_t = jnp.dot(dt_t * a_all, tri.astype(F32), preferred_element_type=F32, precision=HIGHEST)
        stacked = jnp.concatenate([dt_t, acs_t, jnp.zeros((q - 2 * n_hd, q), F32)], axis=0)
        cols_all = stacked.T
        mask = (ri >= ci) if direction == 0 else (ri <= ci)
        end = q - 1 if direction == 0 else 0
        ys = []
        new_states = []
        for g in range(SSD_GROUPS):
            xg = xs_ref[pl.ds(base, q), g * LANE:(g + 1) * LANE]
            bg = bc_ref[pl.ds(base, q), g * SSD_STATE:(g + 1) * SSD_STATE]
            cg = bc_ref[pl.ds(base, q), (SSD_GROUPS + g) * SSD_STATE:(SSD_GROUPS + g + 1) * SSD_STATE]
            gram = lax.dot_general(cg, bg, (((1,), (1,)), ((), ())), preferred_element_type=F32)
            heads = [direction * SSD_HEADS + 2 * g, direction * SSD_HEADS + 2 * g + 1]
            a_bc = [jnp.broadcast_to(cols_all[:, n_hd + k:n_hd + k + 1], (q, LANE)) for k in heads]
            d_bc = [jnp.broadcast_to(cols_all[:, k:k + 1], (q, LANE)) for k in heads]
            a_row = [acs_t[k:k + 1, :] for k in heads]
            a_end = [acs_t[k:k + 1, end:end + 1] for k in heads]
            dtp = jnp.where(lo_half, d_bc[0], d_bc[1])
            acp = jnp.where(lo_half, a_bc[0], a_bc[1])
            a_end_p = jnp.where(lo_half1, a_end[0], a_end[1])
            xdt = (xg * dtp).astype(BF16)
            yd = []
            for k in range(2):
                decay = jnp.exp(jnp.where(mask, a_bc[k] - a_row[k], NEG_INF))
                yd.append(jnp.dot((gram * decay).astype(BF16), xdt, preferred_element_type=F32))
            y_diag = jnp.where(lo_half, yd[0], yd[1])
            st = states[g]
            y_off = jnp.dot(cg, st.astype(BF16), preferred_element_type=F32) * jnp.exp(acp)
            xw = (xg * (jnp.exp(a_end_p - acp) * dtp)).astype(BF16)
            bt_cols = slice((2 * SSD_GROUPS + g) * SSD_STATE, (2 * SSD_GROUPS + g + 1) * SSD_STATE)
            if direction == 0:
                bgt = bg.astype(F32).T.astype(BF16)
                bc_ref[pl.ds(base, q), bt_cols] = bgt
            else:
                bgt = bc_ref[pl.ds(base, q), bt_cols]
            upd = jnp.dot(bgt, xw, preferred_element_type=F32)
            new_states.append(st * jnp.exp(a_end_p) + upd)
            ys.append(y_diag + y_off)
        return jnp.concatenate(ys, axis=-1), tuple(new_states)

    def finish(y, c, xs_ref, z_ref, out_ref):
        base = pl.multiple_of(c * q, q)
        y = y + dvec_ref[...] * xs_ref[pl.ds(base, q), :]
        y = y * _silu(z_ref[pl.ds(base, q), :].astype(F32))
        out_ref[pl.ds(base, q), :] = (_rms(y) * nw_ref[...]).astype(out_ref.dtype)

    n_c = xc_ref.shape[0] // q
    n_l = xl_ref.shape[0] // q

    def fwd_pass(raw_ref, dt_ref, n_chunks, rope, xs_ref, bc_ref, yacc_ref, states):
        def body(c, states):
            prep(raw_ref, c, n_chunks, rope, xs_ref, bc_ref)
            base = pl.multiple_of(c * q, q)
            y, states = scan_chunk(c, 0, dt_ref, xs_ref, bc_ref, states)
            yacc_ref[pl.ds(base, q), :] = y
            return states

        return lax.fori_loop(0, n_chunks, body, states, unroll=min(n_chunks, SSD_UNROLL_FWD))

    def bwd_pass(dt_ref, n_chunks, xs_ref, bc_ref, yacc_ref, z_ref, out_ref, states):
        def body(t, states):
            c = n_chunks - 1 - t
            base = pl.multiple_of(c * q, q)
            y, states = scan_chunk(c, 1, dt_ref, xs_ref, bc_ref, states)
            finish(y + yacc_ref[pl.ds(base, q), :], c, xs_ref, z_ref, out_ref)
            return states

        return lax.fori_loop(0, n_chunks, body, states, unroll=min(n_chunks, SSD_UNROLL_BWD))

    zero_states = tuple(jnp.zeros((SSD_STATE, LANE), F32) for _ in range(SSD_GROUPS))
    states = fwd_pass(xc_ref, dtc_ref, n_c, False, xs_c, bc_c, yacc_c, zero_states)
    fwd_pass(xl_ref, dtl_ref, n_l, True, xs_l, bc_l, yacc_l, states)
    states = bwd_pass(dtc_ref, n_c, xs_c, bc_c, yacc_c, zc_ref, yc_ref, zero_states)
    bwd_pass(dtl_ref, n_l, xs_l, bc_l, yacc_l, zl_ref, yl_ref, states)


def _ssd(u, dt, u_ctx, dt_ctx, conv_w, conv_b, dt_bias, a_log, d_vec, norm_w, cos_t, sin_t):
    b, l, _ = u.shape
    n_ctx = u_ctx.shape[1]
    xb = COL_XBC // XBC_DIM
    zb = COL_ZSS // SSD_DIM
    n_bc = 3 * SSD_GROUPS * SSD_STATE
    const = lambda shape: pl.BlockSpec(shape, lambda i: (0,) * len(shape))
    return pl.pallas_call(
        _ssd_kernel,
        out_shape=(jax.ShapeDtypeStruct((b, l, SSD_DIM), BF16), jax.ShapeDtypeStruct((b, n_ctx, SSD_DIM), BF16)),
        grid=(b,),
        in_specs=[
            pl.BlockSpec((None, l, XBC_DIM), lambda i: (i, 0, xb)),
            pl.BlockSpec((None, l, SSD_DIM), lambda i: (i, 0, zb)),
            pl.BlockSpec((None, 2 * SSD_HEADS, l), lambda i: (i, 0, 0)),
            pl.BlockSpec((None, n_ctx, XBC_DIM), lambda i: (i, 0, xb)),
            pl.BlockSpec((None, n_ctx, SSD_DIM), lambda i: (i, 0, zb)),
            pl.BlockSpec((None, 2 * SSD_HEADS, n_ctx), lambda i: (i, 0, 0)),
            const((3, XBC_DIM)),
            const((1, XBC_DIM)),
            const((2 * SSD_HEADS, SSD_CHUNK)),
            const((2 * SSD_HEADS, SSD_CHUNK)),
            const((1, SSD_DIM)),
            const((1, SSD_DIM)),
            const((l, SSD_STATE)),
            const((l, SSD_STATE)),
        ],
        out_specs=(
            pl.BlockSpec((None, l, SSD_DIM), lambda i: (i, 0, 0)),
            pl.BlockSpec((None, n_ctx, SSD_DIM), lambda i: (i, 0, 0)),
        ),
        scratch_shapes=[
            pltpu.VMEM((l, SSD_DIM), F32),
            pltpu.VMEM((l, n_bc), BF16),
            pltpu.VMEM((n_ctx, SSD_DIM), F32),
            pltpu.VMEM((n_ctx, n_bc), BF16),
            pltpu.VMEM((l, SSD_DIM), F32),
            pltpu.VMEM((n_ctx, SSD_DIM), F32),
        ],
        compiler_params=_params("arbitrary"),
        name="ssd",
    )(u, u, dt, u_ctx, u_ctx, dt_ctx, conv_w, conv_b, dt_bias, a_log, d_vec, norm_w, cos_t, sin_t)


def _rope_tables(seq):
    n_freq = SSD_STATE // 4
    pos = jnp.arange(seq)
    row_pos = (pos // GRID_W).astype(F32)
    col_pos = (pos % GRID_W).astype(F32)
    inv_freq = ROPE_BASE ** (-jnp.arange(n_freq, dtype=F32) / n_freq)
    ar = row_pos[:, None] * inv_freq
    ac = col_pos[:, None] * inv_freq
    cos_t = jnp.concatenate([jnp.cos(ar), jnp.cos(ar), jnp.cos(ac), jnp.cos(ac)], axis=-1)
    sin_t = jnp.concatenate([-jnp.sin(ar), jnp.sin(ar), -jnp.sin(ac), jnp.sin(ac)], axis=-1)
    return cos_t, sin_t


def _outproj_matmul(j, nt, ua_ref, up_ref, un_ref, yna_ref, zna_ref, yss_ref, cw_ref, w_ref):
    tm = ua_ref.shape[0]
    cd = CONV_DIM
    rows8 = lax.broadcasted_iota(jnp.int32, (8, 1), 0)
    ua = ua_ref[...]
    t = ua[:, 2 * cd:3 * cd].astype(F32) * ua[:, 0:cd].astype(F32)
    up = up_ref[15:16, :]
    un = un_ref[0:1, :]
    tprev = up[:, 2 * cd:3 * cd].astype(F32) * up[:, 0:cd].astype(F32) * jnp.where(j > 0, 1.0, 0.0)
    tnext = un[:, 2 * cd:3 * cd].astype(F32) * un[:, 0:cd].astype(F32) * jnp.where(j < nt - 1, 1.0, 0.0)
    tm1 = pltpu.roll(t, 1, 0)
    tm1 = jnp.concatenate([jnp.where(rows8 == 0, tprev, tm1[0:8]), tm1[8:]], axis=0)
    tp1 = pltpu.roll(t, tm - 1, 0)
    tp1 = jnp.concatenate([tp1[:tm - 8], jnp.where(rows8 == 7, tnext, tp1[tm - 8:])], axis=0)
    conv = tm1 * cw_ref[0:1, :] + t * cw_ref[1:2, :] + tp1 * cw_ref[2:3, :]
    ysc = ua[:, cd:2 * cd].astype(F32) * conv * _silu(ua[:, 3 * cd:4 * cd].astype(F32))
    yna = yna_ref[...].astype(F32) * _silu(zna_ref[...].astype(F32))
    ycat = jnp.concatenate([ysc.astype(BF16), yna.astype(BF16), yss_ref[...]], axis=-1)
    return jnp.dot(ycat, w_ref[...], preferred_element_type=F32)


def _outproj_finish(out, x_ref, gate_ref, gpost_ref):
    return x_ref[...] + gate_ref[...] * (_rms(out) * gpost_ref[...])


def _outproj_tile(j, nt, ua_ref, up_ref, un_ref, yna_ref, zna_ref, yss_ref, x_ref, gate_ref, gpost_ref, cw_ref,
                  w_ref):
    out = _outproj_matmul(j, nt, ua_ref, up_ref, un_ref, yna_ref, zna_ref, yss_ref, cw_ref, w_ref)
    return _outproj_finish(out, x_ref, gate_ref, gpost_ref)


def _outproj_kernel(*refs):
    refs[-1][...] = _outproj_tile(pl.program_id(1), pl.num_programs(1), *refs[:-1])


N_OUTPROJ_IN = 11


def _layer_boundary_kernel(*refs):
    out_in = refs[:N_OUTPROJ_IN]
    shift_ref, scale_ref, g_ref, w_ref, wdt_ref = refs[N_OUTPROJ_IN:N_OUTPROJ_IN + 5]
    x_out_ref, u_ref, dt_ref, dt_scr, x_keep = refs[N_OUTPROJ_IN + 5:]
    j = pl.program_id(1)
    nt = pl.num_programs(1) - 1

    def in_half():
        hb = _prenorm_tile(x_keep[...], shift_ref, scale_ref, g_ref)
        _project_tile(hb, w_ref, wdt_ref, u_ref, dt_ref, dt_scr)

    def out_half():
        x_new = _outproj_tile(j, nt, *out_in)
        x_out_ref[...] = x_new
        x_keep[...] = x_new

    @pl.when(j == 0)
    def _():
        out_half()

    @pl.when(jnp.logical_and(j > 0, j < nt))
    def _():
        in_half()
        out_half()

    @pl.when(j == nt)
    def _():
        in_half()


def _outproj_specs(l, d, tm):
    hb = tm // 16
    n_hb = l // 16
    zb = COL_ZNA // NA_DIM
    last = l // tm - 1
    tile = lambda j: jnp.minimum(j, last)
    return [
        pl.BlockSpec((None, tm, 4 * CONV_DIM), lambda i, j: (i, tile(j), 0)),
        pl.BlockSpec((None, 16, 4 * CONV_DIM), lambda i, j: (i, jnp.maximum(tile(j) * hb - 1, 0), 0)),
        pl.BlockSpec((None, 16, 4 * CONV_DIM), lambda i, j: (i, jnp.minimum((tile(j) + 1) * hb, n_hb - 1), 0)),
        pl.BlockSpec((None, tm, NA_DIM), lambda i, j: (i, tile(j), 0)),
        pl.BlockSpec((None, tm, NA_DIM), lambda i, j: (i, tile(j), zb)),
        pl.BlockSpec((None, tm, SSD_DIM), lambda i, j: (i, tile(j), 0)),
        pl.BlockSpec((None, tm, d), lambda i, j: (i, tile(j), 0)),
        pl.BlockSpec((None, 1, d), lambda i, j: (i, 0, 0)),
        pl.BlockSpec((1, d), lambda i, j: (0, 0)),
        pl.BlockSpec((3, CONV_DIM), lambda i, j: (0, 0)),
        pl.BlockSpec((d, d), lambda i, j: (0, 0)),
    ]


def _outproj(u, y_na, y_ss, x, gate, g_post, conv_w, w_out, tm):
    b, l, d = x.shape
    return pl.pallas_call(
        _outproj_kernel,
        out_shape=jax.ShapeDtypeStruct((b, l, d), F32),
        grid=(b, l // tm),
        in_specs=_outproj_specs(l, d, tm),
        out_specs=pl.BlockSpec((None, tm, d), lambda i, j: (i, j, 0)),
        compiler_params=_params("arbitrary", "arbitrary"),
        name="outproj",
    )(u, u, u, y_na, u, y_ss, x, gate, g_post, conv_w, w_out)


def _layer_boundary(u, y_na, y_ss, x, gate, g_post, conv_w, w_out, shift, scale, g_pre, w_u, w_dt, tm):
    b, l, d = x.shape
    last = l // tm - 1
    return pl.pallas_call(
        _layer_boundary_kernel,
        out_shape=(jax.ShapeDtypeStruct((b, l, d), F32), jax.ShapeDtypeStruct((b, l, U_DIM), BF16),
                   jax.ShapeDtypeStruct((b, 2 * SSD_HEADS, l), F32)),
        grid=(b, l // tm + 1),
        in_specs=_outproj_specs(l, d, tm) + [
            pl.BlockSpec((None, 1, d), lambda i, j: (i, 0, 0)),
            pl.BlockSpec((None, 1, d), lambda i, j: (i, 0, 0)),
            pl.BlockSpec((1, d), lambda i, j: (0, 0)),
            pl.BlockSpec((d, U_DIM), lambda i, j: (0, 0)),
            pl.BlockSpec((d, DT_PAD), lambda i, j: (0, 0)),
        ],
        out_specs=(
            pl.BlockSpec((None, tm, d), lambda i, j: (i, jnp.minimum(j, last), 0)),
            pl.BlockSpec((None, tm, U_DIM), lambda i, j: (i, jnp.maximum(j - 1, 0), 0)),
            pl.BlockSpec((None, 2 * SSD_HEADS, tm), lambda i, j: (i, 0, jnp.maximum(j - 1, 0))),
        ),
        scratch_shapes=[pltpu.VMEM((tm, DT_PAD), F32), pltpu.VMEM((tm, d), F32)],
        compiler_params=_params("arbitrary", "arbitrary"),
        name="layer_boundary",
    )(u, u, u, y_na, u, y_ss, x, gate, g_post, conv_w, w_out, shift, scale, g_pre, w_u, w_dt)


def kernel(x, c, ctx, c_ctx, w_ada, b_ada, g_pre, g_post, w_in, conv_a_w, rpb, ssd_conv_w, ssd_conv_b,
           dt_bias_f, dt_bias_b, a_log_f, a_log_b, ssd_d, ssd_norm_w, w_out):
    depth = w_ada.shape[0]
    bsz, seq, d = x.shape
    n_ctx = ctx.shape[1]
    assert d == D_MODEL and seq % (2 * ATTN_ROWS * GRID_W) == 0 and seq // GRID_W >= ATTN_KEY_ROWS
    assert n_ctx % SSD_CHUNK == 0 and w_in.shape[-1] == U_DIM + 2 * SSD_HEADS
    tm_lat = min(512, seq)
    tm_edge = min(1024, seq)
    tm_ctx = n_ctx

    n_rows = -(-(bsz + 1) // 8) * 8
    cc = jnp.zeros((n_rows, d), F32).at[:bsz].set(c).at[bsz].set(c_ctx)
    mods = _ada_all_layers(cc, w_ada, b_ada)
    cos_t, sin_t = _rope_tables(seq)
    pad8 = DT_PAD - 2 * SSD_HEADS

    def layer_inputs(layer):
        m = mods[layer]
        lat = tuple(m[:bsz, k * d:(k + 1) * d].reshape(bsz, 1, d) for k in range(3))
        ctx_mod = tuple(jnp.broadcast_to(m[bsz, k * d:(k + 1) * d], (bsz, 1, d)) for k in range(3))
        w_u = w_in[layer, :, :U_DIM].astype(BF16)
        w_dt = jnp.pad(w_in[layer, :, U_DIM:], ((0, 0), (0, pad8))).astype(BF16)
        return lat, ctx_mod, w_u, w_dt, g_pre[layer].reshape(1, d)

    x_ctx = ctx
    (shift, scale, gate), (shift_c, scale_c, gate_c), w_u, w_dt, gpre = layer_inputs(0)
    u_lat, dt_lat = _inproj(x, shift, scale, gpre, w_u, w_dt, tm_edge)
    u_ctx, dt_ctx = _inproj(x_ctx, shift_c, scale_c, gpre, w_u, w_dt, tm_ctx)
    for layer in range(depth):
        gpost = g_post[layer].reshape(1, d)
        per_head = lambda f, bwd: jnp.broadcast_to(jnp.concatenate([f, bwd])[:, None], (2 * SSD_HEADS, SSD_CHUNK))
        dt_bias = per_head(dt_bias_f[layer], dt_bias_b[layer])
        a_log = per_head(a_log_f[layer], a_log_b[layer])
        d_vec = jnp.repeat(ssd_d[layer], SSD_HEAD_DIM).reshape(1, SSD_DIM)
        bias = _attn_bias_table(rpb[layer])

        y_na = _attention(u_lat, u_ctx, bias)
        y_ss, y_ss_ctx = _ssd(u_lat, dt_lat, u_ctx, dt_ctx, ssd_conv_w[layer], ssd_conv_b[layer].reshape(1, -1),
                              dt_bias, a_log, d_vec, ssd_norm_w[layer].reshape(1, -1), cos_t, sin_t)
        w_o = w_out[layer].astype(BF16)
        if layer == depth - 1:
            return _outproj(u_lat, y_na, y_ss, x, gate, gpost, conv_a_w[layer], w_o, tm_edge)
        (shift, scale, gate_next), (shift_c, scale_c, gate_c_next), w_u, w_dt, gpre = layer_inputs(layer + 1)
        x, u_lat, dt_lat = _layer_boundary(u_lat, y_na, y_ss, x, gate, gpost, conv_a_w[layer], w_o,
                                           shift, scale, gpre, w_u, w_dt, tm_lat)
        x_ctx, u_ctx, dt_ctx = _ctx_boundary(u_ctx, y_ss_ctx, x_ctx, gate_c, gpost, conv_a_w[layer], w_o,
                                             shift_c, scale_c, gpre, w_u, w_dt)
        gate, gate_c = gate_next, gate_c_next
```

```python
import functools
import math

import jax
import jax.numpy as jnp
import numpy as np
from jax import lax
from jax.experimental import pallas as pl
from jax.experimental.pallas import tpu as pltpu

F32 = jnp.float32
BF16 = jnp.bfloat16
HIGHEST = lax.Precision.HIGHEST

D_MODEL = 1024
GRID_W = 64
EPS = 1e-6
CONV_DIM = 256
NA_HEADS = 8
NA_HEAD_DIM = 64
NA_DIM = NA_HEADS * NA_HEAD_DIM
WIN_H = 8
WIN_W = 16
SSD_HEADS = 4
SSD_HEAD_DIM = 64
SSD_DIM = SSD_HEADS * SSD_HEAD_DIM
SSD_GROUPS = 2
SSD_STATE = 128
SSD_CHUNK = 128
XBC_DIM = SSD_DIM + 2 * SSD_GROUPS * SSD_STATE
ROPE_BASE = 10000.0
U_DIM = 4 * CONV_DIM + 4 * NA_DIM + XBC_DIM + SSD_DIM
DT_PAD = 128
COL_Q = 4 * CONV_DIM
COL_K = COL_Q + NA_DIM
COL_V = COL_K + NA_DIM
COL_ZNA = COL_V + NA_DIM
COL_XBC = COL_ZNA + NA_DIM
COL_ZSS = COL_XBC + XBC_DIM

LANE = 128
ATTN_ROWS = 4
ATTN_KEY_ROWS = 12
ATTN_STRIP = 16
ATTN_SLOTS = 3
ATTN_PAIRS_PER_STEP = 1
SSD_UNROLL_FWD = 4
SSD_UNROLL_BWD = 8
VMEM_LIMIT = 56 * 1024 * 1024

NEG_INF = float("-inf")


def _sigmoid(x):
    return 1.0 / (1.0 + jnp.exp(-x))


def _silu(x):
    return x * _sigmoid(x)


def _softplus(x):
    return jnp.maximum(x, 0.0) + jnp.log1p(jnp.exp(-jnp.abs(x)))


def _rms(x):
    return x * lax.rsqrt(jnp.mean(x * x, axis=-1, keepdims=True) + EPS)


def _params(*sem):
    return pltpu.CompilerParams(dimension_semantics=sem, vmem_limit_bytes=VMEM_LIMIT)


def _ada_kernel(cc_ref, w_ref, b_ref, o_ref):
    s = _silu(cc_ref[...])
    o_ref[...] = jnp.dot(s, w_ref[...], preferred_element_type=F32, precision=HIGHEST) + b_ref[...]


def _ada_all_layers(cc, w_ada, b_ada):
    depth = w_ada.shape[0]
    r = cc.shape[0]
    nblk = 3 * D_MODEL // D_MODEL
    return pl.pallas_call(
        _ada_kernel,
        out_shape=jax.ShapeDtypeStruct((depth, r, 3 * D_MODEL), F32),
        grid=(depth, nblk),
        in_specs=[
            pl.BlockSpec((r, D_MODEL), lambda l, n: (0, 0)),
            pl.BlockSpec((None, D_MODEL, D_MODEL), lambda l, n: (l, 0, n)),
            pl.BlockSpec((None, 1, D_MODEL), lambda l, n: (l, 0, n)),
        ],
        out_specs=pl.BlockSpec((None, r, D_MODEL), lambda l, n: (l, 0, n)),
        compiler_params=_params("arbitrary", "arbitrary"),
        name="ada",
    )(cc, w_ada, b_ada.reshape(depth, 1, 3 * D_MODEL))


def _prenorm_tile(x, shift_ref, scale_ref, g_ref):
    h = _rms(x) * g_ref[...]
    h = h * (1.0 + scale_ref[...]) + shift_ref[...]
    return h.astype(BF16)


def _project_cols(hb, w_ref, u_ref, lo, hi):
    nb = 512
    for n in range(lo, hi, nb):
        u_ref[:, n:n + nb] = jnp.dot(hb, w_ref[:, n:n + nb], preferred_element_type=F32).astype(BF16)


def _project_dt(hb, wdt_ref, dt_ref, dt_scr):
    dt_scr[...] = jnp.dot(hb, wdt_ref[...], preferred_element_type=F32)
    dt_ref[...] = dt_scr[...].T[0:2 * SSD_HEADS, :]


def _project_tile(hb, w_ref, wdt_ref, u_ref, dt_ref, dt_scr):
    _project_cols(hb, w_ref, u_ref, 0, U_DIM)
    _project_dt(hb, wdt_ref, dt_ref, dt_scr)


def _inproj_kernel(x_ref, shift_ref, scale_ref, g_ref, w_ref, wdt_ref, u_ref, dt_ref, dt_scr):
    _project_tile(_prenorm_tile(x_ref[...], shift_ref, scale_ref, g_ref), w_ref, wdt_ref, u_ref, dt_ref, dt_scr)


def _inproj(x, shift, scale, g_pre, w_u, w_dt, tm):
    b, l, d = x.shape
    return pl.pallas_call(
        _inproj_kernel,
        out_shape=(jax.ShapeDtypeStruct((b, l, U_DIM), BF16), jax.ShapeDtypeStruct((b, 2 * SSD_HEADS, l), F32)),
        grid=(b, l // tm),
        in_specs=[
            pl.BlockSpec((None, tm, d), lambda i, j: (i, j, 0)),
            pl.BlockSpec((None, 1, d), lambda i, j: (i, 0, 0)),
            pl.BlockSpec((None, 1, d), lambda i, j: (i, 0, 0)),
            pl.BlockSpec((1, d), lambda i, j: (0, 0)),
            pl.BlockSpec((d, U_DIM), lambda i, j: (0, 0)),
            pl.BlockSpec((d, DT_PAD), lambda i, j: (0, 0)),
        ],
        out_specs=(
            pl.BlockSpec((None, tm, U_DIM), lambda i, j: (i, j, 0)),
            pl.BlockSpec((None, 2 * SSD_HEADS, tm), lambda i, j: (i, 0, j)),
        ),
        scratch_shapes=[pltpu.VMEM((tm, DT_PAD), F32)],
        compiler_params=_params("arbitrary", "arbitrary"),
        name="inproj",
    )(x, shift, scale, g_pre, w_u, w_dt)


def _attn_variants(n_rows):
    return {
        "top": dict(jlo=[0, 0, 0, 0], droff=0),
        "mid": dict(jlo=[0, 1, 2, 3], droff=-WIN_H // 2),
        "bot": dict(jlo=[4, 4, 4, 4], droff=-WIN_H),
    }


def _head_scales():
    lane1 = lax.broadcasted_iota(jnp.int32, (1, LANE), 1)
    scale = NA_HEAD_DIM ** -0.5
    lo = jnp.where(lane1 < NA_HEAD_DIM, scale, 0.0).astype(BF16)
    hi = jnp.where(lane1 >= NA_HEAD_DIM, scale, 0.0).astype(BF16)
    return lo, hi


def _softmax_strip(s_scr, p_scr, l_scr, bias_ref, hh, row0, blocks, n_ctx_blocks, n_win_blocks):
    rows = slice(row0, row0 + ATTN_STRIP)
    brow = (row0 % GRID_W)
    lane = lax.broadcasted_iota(jnp.int32, (ATTN_STRIP, LANE), 1)
    vals = []
    for m, d, keep in blocks:
        s = s_scr[rows, m * LANE:(m + 1) * LANE] + bias_ref[hh, d, brow:brow + ATTN_STRIP, :]
        if keep == "lo":
            s = jnp.where(lane < GRID_W, s, NEG_INF)
        elif keep == "hi":
            s = jnp.where(lane >= GRID_W, s, NEG_INF)
        vals.append((m, s))
    for c in range(n_ctx_blocks):
        m = n_win_blocks + c
        vals.append((m, s_scr[rows, m * LANE:(m + 1) * LANE]))
    mx = vals[0][1]
    for _, s in vals[1:]:
        mx = jnp.maximum(mx, s)
    mx = jnp.max(mx, axis=-1, keepdims=True)
    tot = None
    used = set()
    for m, s in vals:
        p = jnp.exp(s - mx)
        tot = p if tot is None else tot + p
        p_scr[rows, m * LANE:(m + 1) * LANE] = p.astype(BF16)
        used.add(m)
    for m in range(n_win_blocks):
        if m not in used:
            p_scr[rows, m * LANE:(m + 1) * LANE] = jnp.zeros((ATTN_STRIP, LANE), BF16)
    l_scr[rows, :] = jnp.broadcast_to(jnp.sum(tot, axis=-1, keepdims=True), (ATTN_STRIP, LANE))


def _attn_kernel(q_ref, k_ref, v_ref, kc_ref, vc_ref, bias_ref, o_ref, *scratch, n_rows):
    n_units = n_rows // ATTN_ROWS
    n_q = ATTN_ROWS * GRID_W
    n_win = ATTN_KEY_ROWS * GRID_W
    n_ctx = kc_ref.shape[0]
    n_win_blocks = n_win // LANE
    n_ctx_blocks = n_ctx // LANE
    variants = _attn_variants(n_rows)
    lane_q = lax.broadcasted_iota(jnp.int32, (n_q, LANE), 1)
    head_scale = _head_scales()
    contract_last = (((1,), (1,)), ((), ()))
    s_bufs, p_bufs, l_bufs = scratch[0::3], scratch[1::3], scratch[2::3]

    def lanes(pr):
        return slice(pr * LANE, (pr + 1) * LANE)

    def key_start(g):
        return min(max(g * ATTN_ROWS - WIN_H // 2, 0), n_rows - ATTN_KEY_ROWS) * GRID_W

    def scores(pr, g, slot):
        s_scr = s_bufs[slot]
        q = q_ref[pl.ds(g * n_q, n_q), lanes(pr)]
        kw = k_ref[pl.ds(key_start(g), n_win), lanes(pr)]
        for hh in range(2):
            qh = q * head_scale[hh]
            hr = slice(hh * n_q, (hh + 1) * n_q)
            s_scr[hr, 0:n_win] = lax.dot_general(qh, kw, contract_last, preferred_element_type=F32)
            s_scr[hr, n_win:n_win + n_ctx] = lax.dot_general(qh, kc_ref[:, lanes(pr)], contract_last,
                                                             preferred_element_type=F32)

    def softmax(pr, kind, slot):
        geo = variants[kind]
        n_parts = GRID_W // ATTN_STRIP
        for hh in range(2):
            for i in range(ATTN_ROWS):
                jlo = geo["jlo"][i]
                blocks = []
                for m in range(n_win_blocks):
                    jl, jr = 2 * m, 2 * m + 1
                    vl = jlo <= jl < jlo + WIN_H
                    vr = jlo <= jr < jlo + WIN_H
                    if not (vl or vr):
                        continue
                    d = (jl - i + geo["droff"]) + WIN_H
                    blocks.append((m, d, None if (vl and vr) else ("lo" if vl else "hi")))
                for part in range(n_parts):
                    row0 = hh * n_q + i * GRID_W + part * ATTN_STRIP
                    _softmax_strip(s_bufs[slot], p_bufs[slot], l_bufs[slot], bias_ref, 2 * pr + hh, row0, blocks,
                                   n_ctx_blocks, n_win_blocks)

    def pv(pr, g, slot):
        p_scr = p_bufs[slot]
        vw = v_ref[pl.ds(key_start(g), n_win), lanes(pr)]
        o2 = jnp.dot(p_scr[:, 0:n_win], vw, preferred_element_type=F32)
        o2 = o2 + jnp.dot(p_scr[:, n_win:n_win + n_ctx], vc_ref[:, lanes(pr)], preferred_element_type=F32)
        o2 = o2 * (1.0 / l_bufs[slot][...])
        out = jnp.where(lane_q < GRID_W, o2[0:n_q], o2[n_q:2 * n_q])
        o_ref[pl.ds(g * n_q, n_q), lanes(pr)] = out.astype(o_ref.dtype)

    def kind(g):
        return "top" if g == 0 else ("bot" if g == n_units - 1 else "mid")

    n_stream = ATTN_PAIRS_PER_STEP * n_units
    for t in range(-1, n_stream + 1):
        if t + 1 < n_stream:
            scores((t + 1) // n_units, (t + 1) % n_units, (t + 1) % ATTN_SLOTS)
        if 0 <= t < n_stream:
            softmax(t // n_units, kind(t % n_units), t % ATTN_SLOTS)
        if 0 <= t - 1:
            pv((t - 1) // n_units, (t - 1) % n_units, (t - 1) % ATTN_SLOTS)


def _attention(u, u_ctx, bias):
    b, l, _ = u.shape
    n_ctx = u_ctx.shape[1]
    n_rows = l // GRID_W
    pps = ATTN_PAIRS_PER_STEP
    width = pps * LANE
    n_q = ATTN_ROWS * GRID_W
    n_keys = ATTN_KEY_ROWS * GRID_W + n_ctx
    qb, kb, vb = COL_Q // width, COL_K // width, COL_V // width
    return pl.pallas_call(
        functools.partial(_attn_kernel, n_rows=n_rows),
        out_shape=jax.ShapeDtypeStruct((b, l, NA_DIM), BF16),
        grid=(b, NA_HEADS // (2 * pps)),
        in_specs=[
            pl.BlockSpec((None, l, width), lambda i, p: (i, 0, qb + p)),
            pl.BlockSpec((None, l, width), lambda i, p: (i, 0, kb + p)),
            pl.BlockSpec((None, l, width), lambda i, p: (i, 0, vb + p)),
            pl.BlockSpec((None, n_ctx, width), lambda i, p: (i, 0, kb + p)),
            pl.BlockSpec((None, n_ctx, width), lambda i, p: (i, 0, vb + p)),
            pl.BlockSpec((2 * pps, 2 * WIN_H, GRID_W, LANE), lambda i, p: (p, 0, 0, 0)),
        ],
        out_specs=pl.BlockSpec((None, l, width), lambda i, p: (i, 0, p)),
        scratch_shapes=ATTN_SLOTS * [
            pltpu.VMEM((2 * n_q, n_keys), F32),
            pltpu.VMEM((2 * n_q, n_keys), BF16),
            pltpu.VMEM((2 * n_q, LANE), F32),
        ],
        compiler_params=_params("arbitrary", "arbitrary"),
        name="natten",
    )(u, u, u, u_ctx, u_ctx, bias)


def _ctx_attn_pair(q, k, v):
    n_q = q.shape[0]
    lane_q = lax.broadcasted_iota(jnp.int32, (n_q, LANE), 1)
    head_scale = _head_scales()
    q2 = jnp.concatenate([q * head_scale[0], q * head_scale[1]], axis=0)
    s = lax.dot_general(q2, k, (((1,), (1,)), ((), ())), preferred_element_type=F32)
    p = jnp.exp(s - jnp.max(s, axis=-1, keepdims=True))
    rinv = 1.0 / jnp.sum(p, axis=-1, keepdims=True)
    o2 = jnp.dot(p.astype(BF16), v, preferred_element_type=F32) * rinv
    return jnp.where(lane_q < GRID_W, o2[0:n_q], o2[n_q:2 * n_q])


def _ctx_boundary_kernel(u_ref, yss_ref, x_ref, gate_ref, gpost_ref, cw_ref, w_out_ref,
                         shift_ref, scale_ref, g_ref, w_ref, wdt_ref,
                         x_out_ref, u_out_ref, dt_ref, dt_scr, yna_scr):
    for p in range(NA_HEADS // 2):
        lanes = slice(p * LANE, (p + 1) * LANE)
        q = u_ref[:, COL_Q + p * LANE:COL_Q + (p + 1) * LANE]
        k = u_ref[:, COL_K + p * LANE:COL_K + (p + 1) * LANE]
        v = u_ref[:, COL_V + p * LANE:COL_V + (p + 1) * LANE]
        yna_scr[:, lanes] = _ctx_attn_pair(q, k, v).astype(yna_scr.dtype)
    conv_cols = u_ref.at[:, 0:4 * CONV_DIM]
    halo = u_ref.at[0:16, 0:4 * CONV_DIM]
    x_new = _outproj_tile(0, 1, conv_cols, halo, halo, yna_scr, u_ref.at[:, COL_ZNA:COL_ZNA + NA_DIM], yss_ref,
                          x_ref, gate_ref, gpost_ref, cw_ref, w_out_ref)
    x_out_ref[...] = x_new
    _project_tile(_prenorm_tile(x_new, shift_ref, scale_ref, g_ref), w_ref, wdt_ref, u_out_ref, dt_ref, dt_scr)


def _ctx_boundary(u_ctx, y_ss, x_ctx, gate, g_post, conv_w, w_out, shift, scale, g_pre, w_u, w_dt):
    b, n, d = x_ctx.shape
    per_batch = lambda *shape: pl.BlockSpec((None,) + shape, lambda i: (i,) + (0,) * len(shape))
    const = lambda *shape: pl.BlockSpec(shape, lambda i: (0,) * len(shape))
    return pl.pallas_call(
        _ctx_boundary_kernel,
        out_shape=(jax.ShapeDtypeStruct((b, n, d), F32), jax.ShapeDtypeStruct((b, n, U_DIM), BF16),
                   jax.ShapeDtypeStruct((b, 2 * SSD_HEADS, n), F32)),
        grid=(b,),
        in_specs=[per_batch(n, U_DIM), per_batch(n, SSD_DIM), per_batch(n, d), per_batch(1, d), const(1, d),
                  const(3, CONV_DIM), const(d, d), per_batch(1, d), per_batch(1, d), const(1, d),
                  const(d, U_DIM), const(d, DT_PAD)],
        out_specs=(per_batch(n, d), per_batch(n, U_DIM), per_batch(2 * SSD_HEADS, n)),
        scratch_shapes=[pltpu.VMEM((n, DT_PAD), F32), pltpu.VMEM((n, NA_DIM), BF16)],
        compiler_params=_params("arbitrary"),
        name="ctx_boundary",
    )(u_ctx, y_ss, x_ctx, gate, g_post, conv_w, w_out, shift, scale, g_pre, w_u, w_dt)


def _attn_bias_table(rpb):
    heads = rpb.shape[0]
    cols = np.arange(GRID_W)
    c0 = np.clip(cols - WIN_W // 2, 0, GRID_W - WIN_W)
    rel = cols[None, :] - cols[:, None] + (WIN_W - 1)
    inside = (cols[None, :] >= c0[:, None]) & (cols[None, :] < c0[:, None] + WIN_W)
    select = ((rel[None] == np.arange(2 * WIN_W - 1)[:, None, None]) & inside[None]).astype(np.float32)
    blk = jnp.einsum("hrj,jck->hrck", rpb.astype(F32), jnp.asarray(select), precision=HIGHEST)
    blk = jnp.where(jnp.asarray(inside)[None, None], blk, NEG_INF)
    pad = jnp.full((heads, 1, GRID_W, GRID_W), NEG_INF, F32)
    left = jnp.concatenate([pad, blk], axis=1)
    right = jnp.concatenate([blk, pad], axis=1)
    return jnp.concatenate([left, right], axis=-1).astype(F32)


def _ssd_kernel(xl_ref, zl_ref, dtl_ref, xc_ref, zc_ref, dtc_ref, cw_ref, cb_ref, dtb_ref, alog_ref,
                dvec_ref, nw_ref, cos_ref, sin_ref, yl_ref, yc_ref,
                xs_l, bc_l, xs_c, bc_c, yacc_l, yacc_c):
    q = SSD_CHUNK
    ri = lax.broadcasted_iota(jnp.int32, (q, q), 0)
    ci = lax.broadcasted_iota(jnp.int32, (q, q), 1)
    lane = lax.broadcasted_iota(jnp.int32, (q, LANE), 1)
    lane1 = lax.broadcasted_iota(jnp.int32, (1, LANE), 1)
    rows8 = lax.broadcasted_iota(jnp.int32, (8, 1), 0)
    lo_half = lane < SSD_HEAD_DIM
    lo_half1 = lane1 < SSD_HEAD_DIM
    a_all = -jnp.exp(alog_ref[...])
    n_hd = 2 * SSD_HEADS

    def prep(raw_ref, c, n_chunks, rope, xs_ref, bc_ref):
        base = pl.multiple_of(c * q, q)
        seq = n_chunks * q
        x = raw_ref[pl.ds(base, q), :].astype(F32)
        pstart = pl.multiple_of(jnp.maximum(base - 16, 0), 16)
        nstart = pl.multiple_of(jnp.minimum(base + q, seq - 16), 16)
        prev = raw_ref[pl.ds(pstart, 16), :][15:16, :].astype(F32) * jnp.where(c > 0, 1.0, 0.0)
        nxt = raw_ref[pl.ds(nstart, 16), :][0:1, :].astype(F32) * jnp.where(c < n_chunks - 1, 1.0, 0.0)
        xm1 = pltpu.roll(x, 1, 0)
        xm1 = jnp.concatenate([jnp.where(rows8 == 0, prev, xm1[0:8]), xm1[8:]], axis=0)
        xp1 = pltpu.roll(x, q - 1, 0)
        xp1 = jnp.concatenate([xp1[:q - 8], jnp.where(rows8 == 7, nxt, xp1[q - 8:])], axis=0)
        y = xm1 * cw_ref[0:1, :] + x * cw_ref[1:2, :] + xp1 * cw_ref[2:3, :] + cb_ref[...]
        y = _silu(y)
        xs_ref[pl.ds(base, q), :] = y[:, 0:SSD_DIM]
        for t in range(2 * SSD_GROUPS):
            blk = y[:, SSD_DIM + t * SSD_STATE:SSD_DIM + (t + 1) * SSD_STATE]
            if rope:
                sw = jnp.where(jnp.bitwise_and(lane, 63) < 32, pltpu.roll(blk, 96, 1), pltpu.roll(blk, 32, 1))
                blk = blk * cos_ref[pl.ds(base, q), :] + sw * sin_ref[pl.ds(base, q), :]
            bc_ref[pl.ds(base, q), t * SSD_STATE:(t + 1) * SSD_STATE] = blk.astype(BF16)

    def scan_chunk(c, direction, dt_ref, xs_ref, bc_ref, states):
        base = pl.multiple_of(c * q, q)
        dt_t = _softplus(dt_ref[:, pl.ds(base, q)] + dtb_ref[...])
        tri = jnp.where(ri <= ci, 1.0, 0.0) if direction == 0 else jnp.where(ri >= ci, 1.0, 0.0)
        acs_t = jnp.dot(dt_t * a_all, tri.astype(F32), preferred_element_type=F32, precision=HIGHEST)
        stacked = jnp.concatenate([dt_t, acs_t, jnp.zeros((q - 2 * n_hd, q), F32)], axis=0)
        cols_all = stacked.T
        mask = (ri >= ci) if direction == 0 else (ri <= ci)
        end = q - 1 if direction == 0 else 0
        ys = []
        new_states = []
        for g in range(SSD_GROUPS):
            xg = xs_ref[pl.ds(base, q), g * LANE:(g + 1) * LANE]
            bg = bc_ref[pl.ds(base, q), g * SSD_STATE:(g + 1) * SSD_STATE]
            cg = bc_ref[pl.ds(base, q), (SSD_GROUPS + g) * SSD_STATE:(SSD_GROUPS + g + 1) * SSD_STATE]
            gram = lax.dot_general(cg, bg, (((1,), (1,)), ((), ())), preferred_element_type=F32)
            heads = [direction * SSD_HEADS + 2 * g, direction * SSD_HEADS + 2 * g + 1]
            a_bc = [jnp.broadcast_to(cols_all[:, n_hd + k:n_hd + k + 1], (q, LANE)) for k in heads]
            d_bc = [jnp.broadcast_to(cols_all[:, k:k + 1], (q, LANE)) for k in heads]
            a_row = [acs_t[k:k + 1, :] for k in heads]
            a_end = [acs_t[k:k + 1, end:end + 1] for k in heads]
            dtp = jnp.where(lo_half, d_bc[0], d_bc[1])
            acp = jnp.where(lo_half, a_bc[0], a_bc[1])
            a_end_p = jnp.where(lo_half1, a_end[0], a_end[1])
            xdt = (xg * dtp).astype(BF16)
            yd = []
            for k in range(2):
                decay = jnp.exp(jnp.where(mask, a_bc[k] - a_row[k], NEG_INF))
                yd.append(jnp.dot((gram * decay).astype(BF16), xdt, preferred_element_type=F32))
            y_diag = jnp.where(lo_half, yd[0], yd[1])
            st = states[g]
            y_off = jnp.dot(cg, st.astype(BF16), preferred_element_type=F32) * jnp.exp(acp)
            xw = (xg * (jnp.exp(a_end_p - acp) * dtp)).astype(BF16)
            bt_cols = slice((2 * SSD_GROUPS + g) * SSD_STATE, (2 * SSD_GROUPS + g + 1) * SSD_STATE)
            if direction == 0:
                bgt = bg.astype(F32).T.astype(BF16)
                bc_ref[pl.ds(base, q), bt_cols] = bgt
            else:
                bgt = bc_ref[pl.ds(base, q), bt_cols]
            upd = jnp.dot(bgt, xw, preferred_element_type=F32)
            new_states.append(st * jnp.exp(a_end_p) + upd)
            ys.append(y_diag + y_off)
        return jnp.concatenate(ys, axis=-1), tuple(new_states)

    def finish(y, c, xs_ref, z_ref, out_ref):
        base = pl.multiple_of(c * q, q)
        y = y + dvec_ref[...] * xs_ref[pl.ds(base, q), :]
        y = y * _silu(z_ref[pl.ds(base, q), :].astype(F32))
        out_ref[pl.ds(base, q), :] = (_rms(y) * nw_ref[...]).astype(out_ref.dtype)

    n_c = xc_ref.shape[0] // q
    n_l = xl_ref.shape[0] // q

    def fwd_pass(raw_ref, dt_ref, n_chunks, rope, xs_ref, bc_ref, yacc_ref, states):
        def body(c, states):
            prep(raw_ref, c, n_chunks, rope, xs_ref, bc_ref)
            base = pl.multiple_of(c * q, q)
            y, states = scan_chunk(c, 0, dt_ref, xs_ref, bc_ref, states)
            yacc_ref[pl.ds(base, q), :] = y
            return states

        return lax.fori_loop(0, n_chunks, body, states, unroll=min(n_chunks, SSD_UNROLL_FWD))

    def bwd_pass(dt_ref, n_chunks, xs_ref, bc_ref, yacc_ref, z_ref, out_ref, states):
        def body(t, states):
            c = n_chunks - 1 - t
            base = pl.multiple_of(c * q, q)
            y, states = scan_chunk(c, 1, dt_ref, xs_ref, bc_ref, states)
            finish(y + yacc_ref[pl.ds(base, q), :], c, xs_ref, z_ref, out_ref)
            return states

        return lax.fori_loop(0, n_chunks, body, states, unroll=min(n_chunks, SSD_UNROLL_BWD))

    zero_states = tuple(jnp.zeros((SSD_STATE, LANE), F32) for _ in range(SSD_GROUPS))
    states = fwd_pass(xc_ref, dtc_ref, n_c, False, xs_c, bc_c, yacc_c, zero_states)
    fwd_pass(xl_ref, dtl_ref, n_l, True, xs_l, bc_l, yacc_l, states)
    states = bwd_pass(dtc_ref, n_c, xs_c, bc_c, yacc_c, zc_ref, yc_ref, zero_states)
    bwd_pass(dtl_ref, n_l, xs_l, bc_l, yacc_l, zl_ref, yl_ref, states)


def _ssd(u, dt, u_ctx, dt_ctx, conv_w, conv_b, dt_bias, a_log, d_vec, norm_w, cos_t, sin_t):
    b, l, _ = u.shape
    n_ctx = u_ctx.shape[1]
    xb = COL_XBC // XBC_DIM
    zb = COL_ZSS // SSD_DIM
    n_bc = 3 * SSD_GROUPS * SSD_STATE
    const = lambda shape: pl.BlockSpec(shape, lambda i: (0,) * len(shape))
    return pl.pallas_call(
        _ssd_kernel,
        out_shape=(jax.ShapeDtypeStruct((b, l, SSD_DIM), BF16), jax.ShapeDtypeStruct((b, n_ctx, SSD_DIM), BF16)),
        grid=(b,),
        in_specs=[
            pl.BlockSpec((None, l, XBC_DIM), lambda i: (i, 0, xb)),
            pl.BlockSpec((None, l, SSD_DIM), lambda i: (i, 0, zb)),
            pl.BlockSpec((None, 2 * SSD_HEADS, l), lambda i: (i, 0, 0)),
            pl.BlockSpec((None, n_ctx, XBC_DIM), lambda i: (i, 0, xb)),
            pl.BlockSpec((None, n_ctx, SSD_DIM), lambda i: (i, 0, zb)),
            pl.BlockSpec((None, 2 * SSD_HEADS, n_ctx), lambda i: (i, 0, 0)),
            const((3, XBC_DIM)),
            const((1, XBC_DIM)),
            const((2 * SSD_HEADS, SSD_CHUNK)),
            const((2 * SSD_HEADS, SSD_CHUNK)),
            const((1, SSD_DIM)),
            const((1, SSD_DIM)),
            const((l, SSD_STATE)),
            const((l, SSD_STATE)),
        ],
        out_specs=(
            pl.BlockSpec((None, l, SSD_DIM), lambda i: (i, 0, 0)),
            pl.BlockSpec((None, n_ctx, SSD_DIM), lambda i: (i, 0, 0)),
        ),
        scratch_shapes=[
            pltpu.VMEM((l, SSD_DIM), F32),
            pltpu.VMEM((l, n_bc), BF16),
            pltpu.VMEM((n_ctx, SSD_DIM), F32),
            pltpu.VMEM((n_ctx, n_bc), BF16),
            pltpu.VMEM((l, SSD_DIM), F32),
            pltpu.VMEM((n_ctx, SSD_DIM), F32),
        ],
        compiler_params=_params("arbitrary"),
        name="ssd",
    )(u, u, dt, u_ctx, u_ctx, dt_ctx, conv_w, conv_b, dt_bias, a_log, d_vec, norm_w, cos_t, sin_t)


def _rope_tables(seq):
    n_freq = SSD_STATE // 4
    pos = jnp.arange(seq)
    row_pos = (pos // GRID_W).astype(F32)
    col_pos = (pos % GRID_W).astype(F32)
    inv_freq = ROPE_BASE ** (-jnp.arange(n_freq, dtype=F32) / n_freq)
    ar = row_pos[:, None] * inv_freq
    ac = col_pos[:, None] * inv_freq
    cos_t = jnp.concatenate([jnp.cos(ar), jnp.cos(ar), jnp.cos(ac), jnp.cos(ac)], axis=-1)
    sin_t = jnp.concatenate([-jnp.sin(ar), jnp.sin(ar), -jnp.sin(ac), jnp.sin(ac)], axis=-1)
    return cos_t, sin_t


def _outproj_matmul(j, nt, ua_ref, up_ref, un_ref, yna_ref, zna_ref, yss_ref, cw_ref, w_ref):
    tm = ua_ref.shape[0]
    cd = CONV_DIM
    rows8 = lax.broadcasted_iota(jnp.int32, (8, 1), 0)
    ua = ua_ref[...]
    t = ua[:, 2 * cd:3 * cd].astype(F32) * ua[:, 0:cd].astype(F32)
    up = up_ref[15:16, :]
    un = un_ref[0:1, :]
    tprev = up[:, 2 * cd:3 * cd].astype(F32) * up[:, 0:cd].astype(F32) * jnp.where(j > 0, 1.0, 0.0)
    tnext = un[:, 2 * cd:3 * cd].astype(F32) * un[:, 0:cd].astype(F32) * jnp.where(j < nt - 1, 1.0, 0.0)
    tm1 = pltpu.roll(t, 1, 0)
    tm1 = jnp.concatenate([jnp.where(rows8 == 0, tprev, tm1[0:8]), tm1[8:]], axis=0)
    tp1 = pltpu.roll(t, tm - 1, 0)
    tp1 = jnp.concatenate([tp1[:tm - 8], jnp.where(rows8 == 7, tnext, tp1[tm - 8:])], axis=0)
    conv = tm1 * cw_ref[0:1, :] + t * cw_ref[1:2, :] + tp1 * cw_ref[2:3, :]
    ysc = ua[:, cd:2 * cd].astype(F32) * conv * _silu(ua[:, 3 * cd:4 * cd].astype(F32))
    yna = yna_ref[...].astype(F32) * _silu(zna_ref[...].astype(F32))
    ycat = jnp.concatenate([ysc.astype(BF16), yna.astype(BF16), yss_ref[...]], axis=-1)
    return jnp.dot(ycat, w_ref[...], preferred_element_type=F32)


def _outproj_finish(out, x_ref, gate_ref, gpost_ref):
    return x_ref[...] + gate_ref[...] * (_rms(out) * gpost_ref[...])


def _outproj_tile(j, nt, ua_ref, up_ref, un_ref, yna_ref, zna_ref, yss_ref, x_ref, gate_ref, gpost_ref, cw_ref,
                  w_ref):
    out = _outproj_matmul(j, nt, ua_ref, up_ref, un_ref, yna_ref, zna_ref, yss_ref, cw_ref, w_ref)
    return _outproj_finish(out, x_ref, gate_ref, gpost_ref)


def _outproj_kernel(*refs):
    refs[-1][...] = _outproj_tile(pl.program_id(1), pl.num_programs(1), *refs[:-1])


N_OUTPROJ_IN = 11


def _layer_boundary_kernel(*refs):
    out_in = refs[:N_OUTPROJ_IN]
    shift_ref, scale_ref, g_ref, w_ref, wdt_ref = refs[N_OUTPROJ_IN:N_OUTPROJ_IN + 5]
    x_out_ref, u_ref, dt_ref, dt_scr, x_keep = refs[N_OUTPROJ_IN + 5:]
    j = pl.program_id(1)
    nt = pl.num_programs(1) - 1

    def in_half():
        hb = _prenorm_tile(x_keep[...], shift_ref, scale_ref, g_ref)
        _project_tile(hb, w_ref, wdt_ref, u_ref, dt_ref, dt_scr)

    def out_half():
        x_new = _outproj_tile(j, nt, *out_in)
        x_out_ref[...] = x_new
        x_keep[...] = x_new

    @pl.when(j == 0)
    def _():
        out_half()

    @pl.when(jnp.logical_and(j > 0, j < nt))
    def _():
        in_half()
        out_half()

    @pl.when(j == nt)
    def _():
        in_half()


def _outproj_specs(l, d, tm):
    hb = tm // 16
    n_hb = l // 16
    zb = COL_ZNA // NA_DIM
    last = l // tm - 1
    tile = lambda j: jnp.minimum(j, last)
    return [
        pl.BlockSpec((None, tm, 4 * CONV_DIM), lambda i, j: (i, tile(j), 0)),
        pl.BlockSpec((None, 16, 4 * CONV_DIM), lambda i, j: (i, jnp.maximum(tile(j) * hb - 1, 0), 0)),
        pl.BlockSpec((None, 16, 4 * CONV_DIM), lambda i, j: (i, jnp.minimum((tile(j) + 1) * hb, n_hb - 1), 0)),
        pl.BlockSpec((None, tm, NA_DIM), lambda i, j: (i, tile(j), 0)),
        pl.BlockSpec((None, tm, NA_DIM), lambda i, j: (i, tile(j), zb)),
        pl.BlockSpec((None, tm, SSD_DIM), lambda i, j: (i, tile(j), 0)),
        pl.BlockSpec((None, tm, d), lambda i, j: (i, tile(j), 0)),
        pl.BlockSpec((None, 1, d), lambda i, j: (i, 0, 0)),
        pl.BlockSpec((1, d), lambda i, j: (0, 0)),
        pl.BlockSpec((3, CONV_DIM), lambda i, j: (0, 0)),
        pl.BlockSpec((d, d), lambda i, j: (0, 0)),
    ]


def _outproj(u, y_na, y_ss, x, gate, g_post, conv_w, w_out, tm):
    b, l, d = x.shape
    return pl.pallas_call(
        _outproj_kernel,
        out_shape=jax.ShapeDtypeStruct((b, l, d), F32),
        grid=(b, l // tm),
        in_specs=_outproj_specs(l, d, tm),
        out_specs=pl.BlockSpec((None, tm, d), lambda i, j: (i, j, 0)),
        compiler_params=_params("arbitrary", "arbitrary"),
        name="outproj",
    )(u, u, u, y_na, u, y_ss, x, gate, g_post, conv_w, w_out)


def _layer_boundary(u, y_na, y_ss, x, gate, g_post, conv_w, w_out, shift, scale, g_pre, w_u, w_dt, tm):
    b, l, d = x.shape
    last = l // tm - 1
    return pl.pallas_call(
        _layer_boundary_kernel,
        out_shape=(jax.ShapeDtypeStruct((b, l, d), F32), jax.ShapeDtypeStruct((b, l, U_DIM), BF16),
                   jax.ShapeDtypeStruct((b, 2 * SSD_HEADS, l), F32)),
        grid=(b, l // tm + 1),
        in_specs=_outproj_specs(l, d, tm) + [
            pl.BlockSpec((None, 1, d), lambda i, j: (i, 0, 0)),
            pl.BlockSpec((None, 1, d), lambda i, j: (i, 0, 0)),
            pl.BlockSpec((1, d), lambda i, j: (0, 0)),
            pl.BlockSpec((d, U_DIM), lambda i, j: (0, 0)),
            pl.BlockSpec((d, DT_PAD), lambda i, j: (0, 0)),
        ],
        out_specs=(
            pl.BlockSpec((None, tm, d), lambda i, j: (i, jnp.minimum(j, last), 0)),
            pl.BlockSpec((None, tm, U_DIM), lambda i, j: (i, jnp.maximum(j - 1, 0), 0)),
            pl.BlockSpec((None, 2 * SSD_HEADS, tm), lambda i, j: (i, 0, jnp.maximum(j - 1, 0))),
        ),
        scratch_shapes=[pltpu.VMEM((tm, DT_PAD), F32), pltpu.VMEM((tm, d), F32)],
        compiler_params=_params("arbitrary", "arbitrary"),
        name="layer_boundary",
    )(u, u, u, y_na, u, y_ss, x, gate, g_post, conv_w, w_out, shift, scale, g_pre, w_u, w_dt)


def kernel(x, c, ctx, c_ctx, w_ada, b_ada, g_pre, g_post, w_in, conv_a_w, rpb, ssd_conv_w, ssd_conv_b,
           dt_bias_f, dt_bias_b, a_log_f, a_log_b, ssd_d, ssd_norm_w, w_out):
    depth = w_ada.shape[0]
    bsz, seq, d = x.shape
    n_ctx = ctx.shape[1]
    assert d == D_MODEL and seq % (2 * ATTN_ROWS * GRID_W) == 0 and seq // GRID_W >= ATTN_KEY_ROWS
    assert n_ctx % SSD_CHUNK == 0 and w_in.shape[-1] == U_DIM + 2 * SSD_HEADS
    tm_lat = min(512, seq)
    tm_edge = min(1024, seq)
    tm_ctx = n_ctx

    n_rows = -(-(bsz + 1) // 8) * 8
    cc = jnp.zeros((n_rows, d), F32).at[:bsz].set(c).at[bsz].set(c_ctx)
    mods = _ada_all_layers(cc, w_ada, b_ada)
    cos_t, sin_t = _rope_tables(seq)
    pad8 = DT_PAD - 2 * SSD_HEADS

    def layer_inputs(layer):
        m = mods[layer]
        lat = tuple(m[:bsz, k * d:(k + 1) * d].reshape(bsz, 1, d) for k in range(3))
        ctx_mod = tuple(jnp.broadcast_to(m[bsz, k * d:(k + 1) * d], (bsz, 1, d)) for k in range(3))
        w_u = w_in[layer, :, :U_DIM].astype(BF16)
        w_dt = jnp.pad(w_in[layer, :, U_DIM:], ((0, 0), (0, pad8))).astype(BF16)
        return lat, ctx_mod, w_u, w_dt, g_pre[layer].reshape(1, d)

    x_ctx = ctx
    (shift, scale, gate), (shift_c, scale_c, gate_c), w_u, w_dt, gpre = layer_inputs(0)
    u_lat, dt_lat = _inproj(x, shift, scale, gpre, w_u, w_dt, tm_edge)
    u_ctx, dt_ctx = _inproj(x_ctx, shift_c, scale_c, gpre, w_u, w_dt, tm_ctx)
    for layer in range(depth):
        gpost = g_post[layer].reshape(1, d)
        per_head = lambda f, bwd: jnp.broadcast_to(jnp.concatenate([f, bwd])[:, None], (2 * SSD_HEADS, SSD_CHUNK))
        dt_bias = per_head(dt_bias_f[layer], dt_bias_b[layer])
        a_log = per_head(a_log_f[layer], a_log_b[layer])
        d_vec = jnp.repeat(ssd_d[layer], SSD_HEAD_DIM).reshape(1, SSD_DIM)
        bias = _attn_bias_table(rpb[layer])

        y_na = _attention(u_lat, u_ctx, bias)
        y_ss, y_ss_ctx = _ssd(u_lat, dt_lat, u_ctx, dt_ctx, ssd_conv_w[layer], ssd_conv_b[layer].reshape(1, -1),
                              dt_bias, a_log, d_vec, ssd_norm_w[layer].reshape(1, -1), cos_t, sin_t)
        w_o = w_out[layer].astype(BF16)
        if layer == depth - 1:
            return _outproj(u_lat, y_na, y_ss, x, gate, gpost, conv_a_w[layer], w_o, tm_edge)
        (shift, scale, gate_next), (shift_c, scale_c, gate_c_next), w_u, w_dt, gpre = layer_inputs(layer + 1)
        x, u_lat, dt_lat = _layer_boundary(u_lat, y_na, y_ss, x, gate, gpost, conv_a_w[layer], w_o,
                                           shift, scale, gpre, w_u, w_dt, tm_lat)
        x_ctx, u_ctx, dt_ctx = _ctx_boundary(u_ctx, y_ss_ctx, x_ctx, gate_c, gpost, conv_a_w[layer], w_o,
                                             shift_c, scale_c, gpre, w_u, w_dt)
        gate, gate_c = gate_next, gate_c_next
```

```python
import functools
import math

import jax
import jax.numpy as jnp
import numpy as np
from jax import lax
from jax.experimental import pallas as pl
from jax.experimental.pallas import tpu as pltpu

F32 = jnp.float32
BF16 = jnp.bfloat16
HIGHEST = lax.Precision.HIGHEST

D_MODEL = 1024
GRID_W = 64
EPS = 1e-6
CONV_DIM = 256
NA_HEADS = 8
NA_HEAD_DIM = 64
NA_DIM = NA_HEADS * NA_HEAD_DIM
WIN_H = 8
WIN_W = 16
SSD_HEADS = 4
SSD_HEAD_DIM = 64
SSD_DIM = SSD_HEADS * SSD_HEAD_DIM
SSD_GROUPS = 2
SSD_STATE = 128
SSD_CHUNK = 128
XBC_DIM = SSD_DIM + 2 * SSD_GROUPS * SSD_STATE
ROPE_BASE = 10000.0
U_DIM = 4 * CONV_DIM + 4 * NA_DIM + XBC_DIM + SSD_DIM
DT_PAD = 128
COL_Q = 4 * CONV_DIM
COL_K = COL_Q + NA_DIM
COL_V = COL_K + NA_DIM
COL_ZNA = COL_V + NA_DIM
COL_XBC = COL_ZNA + NA_DIM
COL_ZSS = COL_XBC + XBC_DIM

LANE = 128
ATTN_ROWS = 4
ATTN_KEY_ROWS = 12
ATTN_STRIP = 16
ATTN_SLOTS = 3
ATTN_PAIRS_PER_STEP = 1
SSD_UNROLL_FWD = 4
SSD_UNROLL_BWD = 8
VMEM_LIMIT = 56 * 1024 * 1024

NEG_INF = float("-inf")


def _sigmoid(x):
    return 1.0 / (1.0 + jnp.exp(-x))


def _silu(x):
    return x * _sigmoid(x)


def _softplus(x):
    return jnp.maximum(x, 0.0) + jnp.log1p(jnp.exp(-jnp.abs(x)))


def _rms(x):
    return x * lax.rsqrt(jnp.mean(x * x, axis=-1, keepdims=True) + EPS)


def _params(*sem):
    return pltpu.CompilerParams(dimension_semantics=sem, vmem_limit_bytes=VMEM_LIMIT)


def _ada_kernel(cc_ref, w_ref, b_ref, o_ref):
    s = _silu(cc_ref[...])
    o_ref[...] = jnp.dot(s, w_ref[...], preferred_element_type=F32, precision=HIGHEST) + b_ref[...]


def _ada_all_layers(cc, w_ada, b_ada):
    depth = w_ada.shape[0]
    r = cc.shape[0]
    nblk = 3 * D_MODEL // D_MODEL
    return pl.pallas_call(
        _ada_kernel,
        out_shape=jax.ShapeDtypeStruct((depth, r, 3 * D_MODEL), F32),
        grid=(depth, nblk),
        in_specs=[
            pl.BlockSpec((r, D_MODEL), lambda l, n: (0, 0)),
            pl.BlockSpec((None, D_MODEL, D_MODEL), lambda l, n: (l, 0, n)),
            pl.BlockSpec((None, 1, D_MODEL), lambda l, n: (l, 0, n)),
        ],
        out_specs=pl.BlockSpec((None, r, D_MODEL), lambda l, n: (l, 0, n)),
        compiler_params=_params("arbitrary", "arbitrary"),
        name="ada",
    )(cc, w_ada, b_ada.reshape(depth, 1, 3 * D_MODEL))


def _prenorm_tile(x, shift_ref, scale_ref, g_ref):
    h = _rms(x) * g_ref[...]
    h = h * (1.0 + scale_ref[...]) + shift_ref[...]
    return h.astype(BF16)


def _project_cols(hb, w_ref, u_ref, lo, hi):
    nb = 512
    for n in range(lo, hi, nb):
        u_ref[:, n:n + nb] = jnp.dot(hb, w_ref[:, n:n + nb], preferred_element_type=F32).astype(BF16)


def _project_dt(hb, wdt_ref, dt_ref, dt_scr):
    dt_scr[...] = jnp.dot(hb, wdt_ref[...], preferred_element_type=F32)
    dt_ref[...] = dt_scr[...].T[0:2 * SSD_HEADS, :]


def _project_tile(hb, w_ref, wdt_ref, u_ref, dt_ref, dt_scr):
    _project_cols(hb, w_ref, u_ref, 0, U_DIM)
    _project_dt(hb, wdt_ref, dt_ref, dt_scr)


def _inproj_kernel(x_ref, shift_ref, scale_ref, g_ref, w_ref, wdt_ref, u_ref, dt_ref, dt_scr):
    _project_tile(_prenorm_tile(x_ref[...], shift_ref, scale_ref, g_ref), w_ref, wdt_ref, u_ref, dt_ref, dt_scr)


def _inproj(x, shift, scale, g_pre, w_u, w_dt, tm):
    b, l, d = x.shape
    return pl.pallas_call(
        _inproj_kernel,
        out_shape=(jax.ShapeDtypeStruct((b, l, U_DIM), BF16), jax.ShapeDtypeStruct((b, 2 * SSD_HEADS, l), F32)),
        grid=(b, l // tm),
        in_specs=[
            pl.BlockSpec((None, tm, d), lambda i, j: (i, j, 0)),
            pl.BlockSpec((None, 1, d), lambda i, j: (i, 0, 0)),
            pl.BlockSpec((None, 1, d), lambda i, j: (i, 0, 0)),
            pl.BlockSpec((1, d), lambda i, j: (0, 0)),
            pl.BlockSpec((d, U_DIM), lambda i, j: (0, 0)),
            pl.BlockSpec((d, DT_PAD), lambda i, j: (0, 0)),
        ],
        out_specs=(
            pl.BlockSpec((None, tm, U_DIM), lambda i, j: (i, j, 0)),
            pl.BlockSpec((None, 2 * SSD_HEADS, tm), lambda i, j: (i, 0, j)),
        ),
        scratch_shapes=[pltpu.VMEM((tm, DT_PAD), F32)],
        compiler_params=_params("arbitrary", "arbitrary"),
        name="inproj",
    )(x, shift, scale, g_pre, w_u, w_dt)


def _attn_variants(n_rows):
    return {
        "top": dict(jlo=[0, 0, 0, 0], droff=0),
        "mid": dict(jlo=[0, 1, 2, 3], droff=-WIN_H // 2),
        "bot": dict(jlo=[4, 4, 4, 4], droff=-WIN_H),
    }


def _head_scales():
    lane1 = lax.broadcasted_iota(jnp.int32, (1, LANE), 1)
    scale = NA_HEAD_DIM ** -0.5
    lo = jnp.where(lane1 < NA_HEAD_DIM, scale, 0.0).astype(BF16)
    hi = jnp.where(lane1 >= NA_HEAD_DIM, scale, 0.0).astype(BF16)
    return lo, hi


def _softmax_strip(s_scr, p_scr, l_scr, bias_ref, hh, row0, blocks, n_ctx_blocks, n_win_blocks):
    rows = slice(row0, row0 + ATTN_STRIP)
    brow = (row0 % GRID_W)
    lane = lax.broadcasted_iota(jnp.int32, (ATTN_STRIP, LANE), 1)
    vals = []
    for m, d, keep in blocks:
        s = s_scr[rows, m * LANE:(m + 1) * LANE] + bias_ref[hh, d, brow:brow + ATTN_STRIP, :]
        if keep == "lo":
            s = jnp.where(lane < GRID_W, s, NEG_INF)
        elif keep == "hi":
            s = jnp.where(lane >= GRID_W, s, NEG_INF)
        vals.append((m, s))
    for c in range(n_ctx_blocks):
        m = n_win_blocks + c
        vals.append((m, s_scr[rows, m * LANE:(m + 1) * LANE]))
    mx = vals[0][1]
    for _, s in vals[1:]:
        mx = jnp.maximum(mx, s)
    mx = jnp.max(mx, axis=-1, keepdims=True)
    tot = None
    used = set()
    for m, s in vals:
        p = jnp.exp(s - mx)
        tot = p if tot is None else tot + p
        p_scr[rows, m * LANE:(m + 1) * LANE] = p.astype(BF16)
        used.add(m)
    for m in range(n_win_blocks):
        if m not in used:
            p_scr[rows, m * LANE:(m + 1) * LANE] = jnp.zeros((ATTN_STRIP, LANE), BF16)
    l_scr[rows, :] = jnp.broadcast_to(jnp.sum(tot, axis=-1, keepdims=True), (ATTN_STRIP, LANE))


def _attn_kernel(q_ref, k_ref, v_ref, kc_ref, vc_ref, bias_ref, o_ref, *scratch, n_rows):
    n_units = n_rows // ATTN_ROWS
    n_q = ATTN_ROWS * GRID_W
    n_win = ATTN_KEY_ROWS * GRID_W
    n_ctx = kc_ref.shape[0]
    n_win_blocks = n_win // LANE
    n_ctx_blocks = n_ctx // LANE
    variants = _attn_variants(n_rows)
    lane_q = lax.broadcasted_iota(jnp.int32, (n_q, LANE), 1)
    head_scale = _head_scales()
    contract_last = (((1,), (1,)), ((), ()))
    s_bufs, p_bufs, l_bufs = scratch[0::3], scratch[1::3], scratch[2::3]

    def lanes(pr):
        return slice(pr * LANE, (pr + 1) * LANE)

    def key_start(g):
        return min(max(g * ATTN_ROWS - WIN_H // 2, 0), n_rows - ATTN_KEY_ROWS) * GRID_W

    def scores(pr, g, slot):
        s_scr = s_bufs[slot]
        q = q_ref[pl.ds(g * n_q, n_q), lanes(pr)]
        kw = k_ref[pl.ds(key_start(g), n_win), lanes(pr)]
        jlos = variants[kind(g)]["jlo"]
        for hh in range(2):
            qh = q * head_scale[hh]
            hr = slice(hh * n_q, (hh + 1) * n_q)
            s_win = lax.dot_general(qh, kw, contract_last, preferred_element_type=F32)
            for i in range(ATTN_ROWS):
                cols = slice((jlos[i] // 2) * LANE, ((jlos[i] + WIN_H - 1) // 2 + 1) * LANE)
                s_scr[hh * n_q + i * GRID_W:hh * n_q + (i + 1) * GRID_W, cols] = s_win[i * GRID_W:(i + 1) * GRID_W, cols]
            s_scr[hr, n_win:n_win + n_ctx] = lax.dot_general(qh, kc_ref[:, lanes(pr)], contract_last,
                                                             preferred_element_type=F32)

    def softmax(pr, kind, slot):
        geo = variants[kind]
        n_parts = GRID_W // ATTN_STRIP
        for hh in range(2):
            for i in range(ATTN_ROWS):
                jlo = geo["jlo"][i]
                blocks = []
                for m in range(n_win_blocks):
                    jl, jr = 2 * m, 2 * m + 1
                    vl = jlo <= jl < jlo + WIN_H
                    vr = jlo <= jr < jlo + WIN_H
                    if not (vl or vr):
                        continue
                    d = (jl - i + geo["droff"]) + WIN_H
                    blocks.append((m, d, None if (vl and vr) else ("lo" if vl else "hi")))
                for part in range(n_parts):
                    row0 = hh * n_q + i * GRID_W + part * ATTN_STRIP
                    _softmax_strip(s_bufs[slot], p_bufs[slot], l_bufs[slot], bias_ref, 2 * pr + hh, row0, blocks,
                                   n_ctx_blocks, n_win_blocks)

    def pv(pr, g, slot):
        p_scr = p_bufs[slot]
        vw = v_ref[pl.ds(key_start(g), n_win), lanes(pr)]
        o2 = jnp.dot(p_scr[:, 0:n_win], vw, preferred_element_type=F32)
        o2 = o2 + jnp.dot(p_scr[:, n_win:n_win + n_ctx], vc_ref[:, lanes(pr)], preferred_element_type=F32)
        o2 = o2 * (1.0 / l_bufs[slot][...])
        out = jnp.where(lane_q < GRID_W, o2[0:n_q], o2[n_q:2 * n_q])
        o_ref[pl.ds(g * n_q, n_q), lanes(pr)] = out.astype(o_ref.dtype)

    def kind(g):
        return "top" if g == 0 else ("bot" if g == n_units - 1 else "mid")

    n_stream = ATTN_PAIRS_PER_STEP * n_units
    for t in range(-1, n_stream + 1):
        if t + 1 < n_stream:
            scores((t + 1) // n_units, (t + 1) % n_units, (t + 1) % ATTN_SLOTS)
        if 0 <= t < n_stream:
            softmax(t // n_units, kind(t % n_units), t % ATTN_SLOTS)
        if 0 <= t - 1:
            pv((t - 1) // n_units, (t - 1) % n_units, (t - 1) % ATTN_SLOTS)


def _attention(u, u_ctx, bias):
    b, l, _ = u.shape
    n_ctx = u_ctx.shape[1]
    n_rows = l // GRID_W
    pps = ATTN_PAIRS_PER_STEP
    width = pps * LANE
    n_q = ATTN_ROWS * GRID_W
    n_keys = ATTN_KEY_ROWS * GRID_W + n_ctx
    qb, kb, vb = COL_Q // width, COL_K // width, COL_V // width
    return pl.pallas_call(
        functools.partial(_attn_kernel, n_rows=n_rows),
        out_shape=jax.ShapeDtypeStruct((b, l, NA_DIM), BF16),
        grid=(b, NA_HEADS // (2 * pps)),
        in_specs=[
            pl.BlockSpec((None, l, width), lambda i, p: (i, 0, qb + p)),
            pl.BlockSpec((None, l, width), lambda i, p: (i, 0, kb + p)),
            pl.BlockSpec((None, l, width), lambda i, p: (i, 0, vb + p)),
            pl.BlockSpec((None, n_ctx, width), lambda i, p: (i, 0, kb + p)),
            pl.BlockSpec((None, n_ctx, width), lambda i, p: (i, 0, vb + p)),
            pl.BlockSpec((2 * pps, 2 * WIN_H, GRID_W, LANE), lambda i, p: (p, 0, 0, 0)),
        ],
        out_specs=pl.BlockSpec((None, l, width), lambda i, p: (i, 0, p)),
        scratch_shapes=ATTN_SLOTS * [
            pltpu.VMEM((2 * n_q, n_keys), F32),
            pltpu.VMEM((2 * n_q, n_keys), BF16),
            pltpu.VMEM((2 * n_q, LANE), F32),
        ],
        compiler_params=_params("arbitrary", "arbitrary"),
        name="natten",
    )(u, u, u, u_ctx, u_ctx, bias)


def _ctx_attn_pair(q, k, v):
    n_q = q.shape[0]
    lane_q = lax.broadcasted_iota(jnp.int32, (n_q, LANE), 1)
    head_scale = _head_scales()
    q2 = jnp.concatenate([q * head_scale[0], q * head_scale[1]], axis=0)
    s = lax.dot_general(q2, k, (((1,), (1,)), ((), ())), preferred_element_type=F32)
    p = jnp.exp(s - jnp.max(s, axis=-1, keepdims=True))
    rinv = 1.0 / jnp.sum(p, axis=-1, keepdims=True)
    o2 = jnp.dot(p.astype(BF16), v, preferred_element_type=F32) * rinv
    return jnp.where(lane_q < GRID_W, o2[0:n_q], o2[n_q:2 * n_q])


def _ctx_boundary_kernel(u_ref, yss_ref, x_ref, gate_ref, gpost_ref, cw_ref, w_out_ref,
                         shift_ref, scale_ref, g_ref, w_ref, wdt_ref,
                         x_out_ref, u_out_ref, dt_ref, dt_scr, yna_scr):
    for p in range(NA_HEADS // 2):
        lanes = slice(p * LANE, (p + 1) * LANE)
        q = u_ref[:, COL_Q + p * LANE:COL_Q + (p + 1) * LANE]
        k = u_ref[:, COL_K + p * LANE:COL_K + (p + 1) * LANE]
        v = u_ref[:, COL_V + p * LANE:COL_V + (p + 1) * LANE]
        yna_scr[:, lanes] = _ctx_attn_pair(q, k, v).astype(yna_scr.dtype)
    conv_cols = u_ref.at[:, 0:4 * CONV_DIM]
    halo = u_ref.at[0:16, 0:4 * CONV_DIM]
    x_new = _outproj_tile(0, 1, conv_cols, halo, halo, yna_scr, u_ref.at[:, COL_ZNA:COL_ZNA + NA_DIM], yss_ref,
                          x_ref, gate_ref, gpost_ref, cw_ref, w_out_ref)
    x_out_ref[...] = x_new
    _project_tile(_prenorm_tile(x_new, shift_ref, scale_ref, g_ref), w_ref, wdt_ref, u_out_ref, dt_ref, dt_scr)


def _ctx_boundary(u_ctx, y_ss, x_ctx, gate, g_post, conv_w, w_out, shift, scale, g_pre, w_u, w_dt):
    b, n, d = x_ctx.shape
    per_batch = lambda *shape: pl.BlockSpec((None,) + shape, lambda i: (i,) + (0,) * len(shape))
    const = lambda *shape: pl.BlockSpec(shape, lambda i: (0,) * len(shape))
    return pl.pallas_call(
        _ctx_boundary_kernel,
        out_shape=(jax.ShapeDtypeStruct((b, n, d), F32), jax.ShapeDtypeStruct((b, n, U_DIM), BF16),
                   jax.ShapeDtypeStruct((b, 2 * SSD_HEADS, n), F32)),
        grid=(b,),
        in_specs=[per_batch(n, U_DIM), per_batch(n, SSD_DIM), per_batch(n, d), per_batch(1, d), const(1, d),
                  const(3, CONV_DIM), const(d, d), per_batch(1, d), per_batch(1, d), const(1, d),
                  const(d, U_DIM), const(d, DT_PAD)],
        out_specs=(per_batch(n, d), per_batch(n, U_DIM), per_batch(2 * SSD_HEADS, n)),
        scratch_shapes=[pltpu.VMEM((n, DT_PAD), F32), pltpu.VMEM((n, NA_DIM), BF16)],
        compiler_params=_params("arbitrary"),
        name="ctx_boundary",
    )(u_ctx, y_ss, x_ctx, gate, g_post, conv_w, w_out, shift, scale, g_pre, w_u, w_dt)


def _attn_bias_table(rpb):
    heads = rpb.shape[0]
    cols = np.arange(GRID_W)
    c0 = np.clip(cols - WIN_W // 2, 0, GRID_W - WIN_W)
    rel = cols[None, :] - cols[:, None] + (WIN_W - 1)
    inside = (cols[None, :] >= c0[:, None]) & (cols[None, :] < c0[:, None] + WIN_W)
    select = ((rel[None] == np.arange(2 * WIN_W - 1)[:, None, None]) & inside[None]).astype(np.float32)
    blk = jnp.einsum("hrj,jck->hrck", rpb.astype(F32), jnp.asarray(select), precision=HIGHEST)
    blk = jnp.where(jnp.asarray(inside)[None, None], blk, NEG_INF)
    pad = jnp.full((heads, 1, GRID_W, GRID_W), NEG_INF, F32)
    left = jnp.concatenate([pad, blk], axis=1)
    right = jnp.concatenate([blk, pad], axis=1)
    return jnp.concatenate([left, right], axis=-1).astype(F32)


def _ssd_kernel(xl_ref, zl_ref, dtl_ref, xc_ref, zc_ref, dtc_ref, cw_ref, cb_ref, dtb_ref, alog_ref,
                dvec_ref, nw_ref, cos_ref, sin_ref, yl_ref, yc_ref,
                xs_l, bc_l, xs_c, bc_c, yacc_l, yacc_c):
    q = SSD_CHUNK
    ri = lax.broadcasted_iota(jnp.int32, (q, q), 0)
    ci = lax.broadcasted_iota(jnp.int32, (q, q), 1)
    lane = lax.broadcasted_iota(jnp.int32, (q, LANE), 1)
    lane1 = lax.broadcasted_iota(jnp.int32, (1, LANE), 1)
    rows8 = lax.broadcasted_iota(jnp.int32, (8, 1), 0)
    lo_half = lane < SSD_HEAD_DIM
    lo_half1 = lane1 < SSD_HEAD_DIM
    a_all = -jnp.exp(alog_ref[...])
    n_hd = 2 * SSD_HEADS

    def prep(raw_ref, c, n_chunks, rope, xs_ref, bc_ref):
        base = pl.multiple_of(c * q, q)
        seq = n_chunks * q
        x = raw_ref[pl.ds(base, q), :].astype(F32)
        pstart = pl.multiple_of(jnp.maximum(base - 16, 0), 16)
        nstart = pl.multiple_of(jnp.minimum(base + q, seq - 16), 16)
        prev = raw_ref[pl.ds(pstart, 16), :][15:16, :].astype(F32) * jnp.where(c > 0, 1.0, 0.0)
        nxt = raw_ref[pl.ds(nstart, 16), :][0:1, :].astype(F32) * jnp.where(c < n_chunks - 1, 1.0, 0.0)
        xm1 = pltpu.roll(x, 1, 0)
        xm1 = jnp.concatenate([jnp.where(rows8 == 0, prev, xm1[0:8]), xm1[8:]], axis=0)
        xp1 = pltpu.roll(x, q - 1, 0)
        xp1 = jnp.concatenate([xp1[:q - 8], jnp.where(rows8 == 7, nxt, xp1[q - 8:])], axis=0)
        y = xm1 * cw_ref[0:1, :] + x * cw_ref[1:2, :] + xp1 * cw_ref[2:3, :] + cb_ref[...]
        y = _silu(y)
        xs_ref[pl.ds(base, q), :] = y[:, 0:SSD_DIM]
        for t in range(2 * SSD_GROUPS):
            blk = y[:, SSD_DIM + t * SSD_STATE:SSD_DIM + (t + 1) * SSD_STATE]
            if rope:
                sw = jnp.where(jnp.bitwise_and(lane, 63) < 32, pltpu.roll(blk, 96, 1), pltpu.roll(blk, 32, 1))
                blk = blk * cos_ref[pl.ds(base, q), :] + sw * sin_ref[pl.ds(base, q), :]
            bc_ref[pl.ds(base, q), t * SSD_STATE:(t + 1) * SSD_STATE] = blk.astype(BF16)

    def scan_chunk(c, direction, dt_ref, xs_ref, bc_ref, states):
        base = pl.multiple_of(c * q, q)
        dt_t = _softplus(dt_ref[:, pl.ds(base, q)] + dtb_ref[...])
        tri = jnp.where(ri <= ci, 1.0, 0.0) if direction == 0 else jnp.where(ri >= ci, 1.0, 0.0)
        acs_t = jnp.dot(dt_t * a_all, tri.astype(F32), preferred_element_type=F32, precision=HIGHEST)
        stacked = jnp.concatenate([dt_t, acs_t, jnp.zeros((q - 2 * n_hd, q), F32)], axis=0)
        cols_all = stacked.T
        mask = (ri >= ci) if direction == 0 else (ri <= ci)
        end = q - 1 if direction == 0 else 0
        ys = []
        new_states = []
        for g in range(SSD_GROUPS):
            xg = xs_ref[pl.ds(base, q), g * LANE:(g + 1) * LANE]
            bg = bc_ref[pl.ds(base, q), g * SSD_STATE:(g + 1) * SSD_STATE]
            cg = bc_ref[pl.ds(base, q), (SSD_GROUPS + g) * SSD_STATE:(SSD_GROUPS + g + 1) * SSD_STATE]
            gram = lax.dot_general(cg, bg, (((1,), (1,)), ((), ())), preferred_element_type=F32)
            heads = [direction * SSD_HEADS + 2 * g, direction * SSD_HEADS + 2 * g + 1]
            a_bc = [jnp.broadcast_to(cols_all[:, n_hd + k:n_hd + k + 1], (q, LANE)) for k in heads]
            d_bc = [jnp.broadcast_to(cols_all[:, k:k + 1], (q, LANE)) for k in heads]
            a_row = [acs_t[k:k + 1, :] for k in heads]
            a_end = [acs_t[k:k + 1, end:end + 1] for k in heads]
            dtp = jnp.where(lo_half, d_bc[0], d_bc[1])
            acp = jnp.where(lo_half, a_bc[0], a_bc[1])
            a_end_p = jnp.where(lo_half1, a_end[0], a_end[1])
            xdt = (xg * dtp).astype(BF16)
            yd = []
            for k in range(2):
                decay = jnp.exp(jnp.where(mask, a_bc[k] - a_row[k], NEG_INF))
                yd.append(jnp.dot((gram * decay).astype(BF16), xdt, preferred_element_type=F32))
            y_diag = jnp.where(lo_half, yd[0], yd[1])
            st = states[g]
            y_off = jnp.dot(cg, st.astype(BF16), preferred_element_type=F32) * jnp.exp(acp)
            xw = (xg * (jnp.exp(a_end_p - acp) * dtp)).astype(BF16)
            bt_cols = slice((2 * SSD_GROUPS + g) * SSD_STATE, (2 * SSD_GROUPS + g + 1) * SSD_STATE)
            if direction == 0:
                bgt = bg.astype(F32).T.astype(BF16)
                bc_ref[pl.ds(base, q), bt_cols] = bgt
            else:
                bgt = bc_ref[pl.ds(base, q), bt_cols]
            upd = jnp.dot(bgt, xw, preferred_element_type=F32)
            new_states.append(st * jnp.exp(a_end_p) + upd)
            ys.append(y_diag + y_off)
        return jnp.concatenate(ys, axis=-1), tuple(new_states)

    def finish(y, c, xs_ref, z_ref, out_ref):
        base = pl.multiple_of(c * q, q)
        y = y + dvec_ref[...] * xs_ref[pl.ds(base, q), :]
        y = y * _silu(z_ref[pl.ds(base, q), :].astype(F32))
        out_ref[pl.ds(base, q), :] = (_rms(y) * nw_ref[...]).astype(out_ref.dtype)

    n_c = xc_ref.shape[0] // q
    n_l = xl_ref.shape[0] // q

    def fwd_pass(raw_ref, dt_ref, n_chunks, rope, xs_ref, bc_ref, yacc_ref, states):
        def body(c, states):
            prep(raw_ref, c, n_chunks, rope, xs_ref, bc_ref)
            base = pl.multiple_of(c * q, q)
            y, states = scan_chunk(c, 0, dt_ref, xs_ref, bc_ref, states)
            yacc_ref[pl.ds(base, q), :] = y
            return states

        return lax.fori_loop(0, n_chunks, body, states, unroll=min(n_chunks, SSD_UNROLL_FWD))

    def bwd_pass(dt_ref, n_chunks, xs_ref, bc_ref, yacc_ref, z_ref, out_ref, states):
        def body(t, states):
            c = n_chunks - 1 - t
            base = pl.multiple_of(c * q, q)
            y, states = scan_chunk(c, 1, dt_ref, xs_ref, bc_ref, states)
            finish(y + yacc_ref[pl.ds(base, q), :], c, xs_ref, z_ref, out_ref)
            return states

        return lax.fori_loop(0, n_chunks, body, states, unroll=min(n_chunks, SSD_UNROLL_BWD))

    zero_states = tuple(jnp.zeros((SSD_STATE, LANE), F32) for _ in range(SSD_GROUPS))
    states = fwd_pass(xc_ref, dtc_ref, n_c, False, xs_c, bc_c, yacc_c, zero_states)
    fwd_pass(xl_ref, dtl_ref, n_l, True, xs_l, bc_l, yacc_l, states)
    states = bwd_pass(dtc_ref, n_c, xs_c, bc_c, yacc_c, zc_ref, yc_ref, zero_states)
    bwd_pass(dtl_ref, n_l, xs_l, bc_l, yacc_l, zl_ref, yl_ref, states)


def _ssd(u, dt, u_ctx, dt_ctx, conv_w, conv_b, dt_bias, a_log, d_vec, norm_w, cos_t, sin_t):
    b, l, _ = u.shape
    n_ctx = u_ctx.shape[1]
    xb = COL_XBC // XBC_DIM
    zb = COL_ZSS // SSD_DIM
    n_bc = 3 * SSD_GROUPS * SSD_STATE
    const = lambda shape: pl.BlockSpec(shape, lambda i: (0,) * len(shape))
    return pl.pallas_call(
        _ssd_kernel,
        out_shape=(jax.ShapeDtypeStruct((b, l, SSD_DIM), BF16), jax.ShapeDtypeStruct((b, n_ctx, SSD_DIM), BF16)),
        grid=(b,),
        in_specs=[
            pl.BlockSpec((None, l, XBC_DIM), lambda i: (i, 0, xb)),
            pl.BlockSpec((None, l, SSD_DIM), lambda i: (i, 0, zb)),
            pl.BlockSpec((None, 2 * SSD_HEADS, l), lambda i: (i, 0, 0)),
            pl.BlockSpec((None, n_ctx, XBC_DIM), lambda i: (i, 0, xb)),
            pl.BlockSpec((None, n_ctx, SSD_DIM), lambda i: (i, 0, zb)),
            pl.BlockSpec((None, 2 * SSD_HEADS, n_ctx), lambda i: (i, 0, 0)),
            const((3, XBC_DIM)),
            const((1, XBC_DIM)),
            const((2 * SSD_HEADS, SSD_CHUNK)),
            const((2 * SSD_HEADS, SSD_CHUNK)),
            const((1, SSD_DIM)),
            const((1, SSD_DIM)),
            const((l, SSD_STATE)),
            const((l, SSD_STATE)),
        ],
        out_specs=(
            pl.BlockSpec((None, l, SSD_DIM), lambda i: (i, 0, 0)),
            pl.BlockSpec((None, n_ctx, SSD_DIM), lambda i: (i, 0, 0)),
        ),
        scratch_shapes=[
            pltpu.VMEM((l, SSD_DIM), F32),
            pltpu.VMEM((l, n_bc), BF16),
            pltpu.VMEM((n_ctx, SSD_DIM), F32),
            pltpu.VMEM((n_ctx, n_bc), BF16),
            pltpu.VMEM((l, SSD_DIM), F32),
            pltpu.VMEM((n_ctx, SSD_DIM), F32),
        ],
        compiler_params=_params("arbitrary"),
        name="ssd",
    )(u, u, dt, u_ctx, u_ctx, dt_ctx, conv_w, conv_b, dt_bias, a_log, d_vec, norm_w, cos_t, sin_t)


def _rope_tables(seq):
    n_freq = SSD_STATE // 4
    pos = jnp.arange(seq)
    row_pos = (pos // GRID_W).astype(F32)
    col_pos = (pos % GRID_W).astype(F32)
    inv_freq = ROPE_BASE ** (-jnp.arange(n_freq, dtype=F32) / n_freq)
    ar = row_pos[:, None] * inv_freq
    ac = col_pos[:, None] * inv_freq
    cos_t = jnp.concatenate([jnp.cos(ar), jnp.cos(ar), jnp.cos(ac), jnp.cos(ac)], axis=-1)
    sin_t = jnp.concatenate([-jnp.sin(ar), jnp.sin(ar), -jnp.sin(ac), jnp.sin(ac)], axis=-1)
    return cos_t, sin_t


def _outproj_matmul(j, nt, ua_ref, up_ref, un_ref, yna_ref, zna_ref, yss_ref, cw_ref, w_ref):
    tm = ua_ref.shape[0]
    cd = CONV_DIM
    rows8 = lax.broadcasted_iota(jnp.int32, (8, 1), 0)
    ua = ua_ref[...]
    t = ua[:, 2 * cd:3 * cd].astype(F32) * ua[:, 0:cd].astype(F32)
    up = up_ref[15:16, :]
    un = un_ref[0:1, :]
    tprev = up[:, 2 * cd:3 * cd].astype(F32) * up[:, 0:cd].astype(F32) * jnp.where(j > 0, 1.0, 0.0)
    tnext = un[:, 2 * cd:3 * cd].astype(F32) * un[:, 0:cd].astype(F32) * jnp.where(j < nt - 1, 1.0, 0.0)
    tm1 = pltpu.roll(t, 1, 0)
    tm1 = jnp.concatenate([jnp.where(rows8 == 0, tprev, tm1[0:8]), tm1[8:]], axis=0)
    tp1 = pltpu.roll(t, tm - 1, 0)
    tp1 = jnp.concatenate([tp1[:tm - 8], jnp.where(rows8 == 7, tnext, tp1[tm - 8:])], axis=0)
    conv = tm1 * cw_ref[0:1, :] + t * cw_ref[1:2, :] + tp1 * cw_ref[2:3, :]
    ysc = ua[:, cd:2 * cd].astype(F32) * conv * _silu(ua[:, 3 * cd:4 * cd].astype(F32))
    yna = yna_ref[...].astype(F32) * _silu(zna_ref[...].astype(F32))
    ycat = jnp.concatenate([ysc.astype(BF16), yna.astype(BF16), yss_ref[...]], axis=-1)
    return jnp.dot(ycat, w_ref[...], preferred_element_type=F32)


def _outproj_finish(out, x_ref, gate_ref, gpost_ref):
    return x_ref[...] + gate_ref[...] * (_rms(out) * gpost_ref[...])


def _outproj_tile(j, nt, ua_ref, up_ref, un_ref, yna_ref, zna_ref, yss_ref, x_ref, gate_ref, gpost_ref, cw_ref,
                  w_ref):
    out = _outproj_matmul(j, nt, ua_ref, up_ref, un_ref, yna_ref, zna_ref, yss_ref, cw_ref, w_ref)
    return _outproj_finish(out, x_ref, gate_ref, gpost_ref)


def _outproj_kernel(*refs):
    refs[-1][...] = _outproj_tile(pl.program_id(1), pl.num_programs(1), *refs[:-1])


N_OUTPROJ_IN = 11


def _layer_boundary_kernel(*refs):
    out_in = refs[:N_OUTPROJ_IN]
    shift_ref, scale_ref, g_ref, w_ref, wdt_ref = refs[N_OUTPROJ_IN:N_OUTPROJ_IN + 5]
    x_out_ref, u_ref, dt_ref, dt_scr, x_keep = refs[N_OUTPROJ_IN + 5:]
    j = pl.program_id(1)
    nt = pl.num_programs(1) - 1

    def in_half():
        hb = _prenorm_tile(x_keep[...], shift_ref, scale_ref, g_ref)
        _project_tile(hb, w_ref, wdt_ref, u_ref, dt_ref, dt_scr)

    def out_half():
        x_new = _outproj_tile(j, nt, *out_in)
        x_out_ref[...] = x_new
        x_keep[...] = x_new

    @pl.when(j == 0)
    def _():
        out_half()

    @pl.when(jnp.logical_and(j > 0, j < nt))
    def _():
        in_half()
        out_half()

    @pl.when(j == nt)
    def _():
        in_half()


def _outproj_specs(l, d, tm):
    hb = tm // 16
    n_hb = l // 16
    zb = COL_ZNA // NA_DIM
    last = l // tm - 1
    tile = lambda j: jnp.minimum(j, last)
    return [
        pl.BlockSpec((None, tm, 4 * CONV_DIM), lambda i, j: (i, tile(j), 0)),
        pl.BlockSpec((None, 16, 4 * CONV_DIM), lambda i, j: (i, jnp.maximum(tile(j) * hb - 1, 0), 0)),
        pl.BlockSpec((None, 16, 4 * CONV_DIM), lambda i, j: (i, jnp.minimum((tile(j) + 1) * hb, n_hb - 1), 0)),
        pl.BlockSpec((None, tm, NA_DIM), lambda i, j: (i, tile(j), 0)),
        pl.BlockSpec((None, tm, NA_DIM), lambda i, j: (i, tile(j), zb)),
        pl.BlockSpec((None, tm, SSD_DIM), lambda i, j: (i, tile(j), 0)),
        pl.BlockSpec((None, tm, d), lambda i, j: (i, tile(j), 0)),
        pl.BlockSpec((None, 1, d), lambda i, j: (i, 0, 0)),
        pl.BlockSpec((1, d), lambda i, j: (0, 0)),
        pl.BlockSpec((3, CONV_DIM), lambda i, j: (0, 0)),
        pl.BlockSpec((d, d), lambda i, j: (0, 0)),
    ]


def _outproj(u, y_na, y_ss, x, gate, g_post, conv_w, w_out, tm):
    b, l, d = x.shape
    return pl.pallas_call(
        _outproj_kernel,
        out_shape=jax.ShapeDtypeStruct((b, l, d), F32),
        grid=(b, l // tm),
        in_specs=_outproj_specs(l, d, tm),
        out_specs=pl.BlockSpec((None, tm, d), lambda i, j: (i, j, 0)),
        compiler_params=_params("arbitrary", "arbitrary"),
        name="outproj",
    )(u, u, u, y_na, u, y_ss, x, gate, g_post, conv_w, w_out)


def _layer_boundary(u, y_na, y_ss, x, gate, g_post, conv_w, w_out, shift, scale, g_pre, w_u, w_dt, tm):
    b, l, d = x.shape
    last = l // tm - 1
    return pl.pallas_call(
        _layer_boundary_kernel,
        out_shape=(jax.ShapeDtypeStruct((b, l, d), F32), jax.ShapeDtypeStruct((b, l, U_DIM), BF16),
                   jax.ShapeDtypeStruct((b, 2 * SSD_HEADS, l), F32)),
        grid=(b, l // tm + 1),
        in_specs=_outproj_specs(l, d, tm) + [
            pl.BlockSpec((None, 1, d), lambda i, j: (i, 0, 0)),
            pl.BlockSpec((None, 1, d), lambda i, j: (i, 0, 0)),
            pl.BlockSpec((1, d), lambda i, j: (0, 0)),
            pl.BlockSpec((d, U_DIM), lambda i, j: (0, 0)),
            pl.BlockSpec((d, DT_PAD), lambda i, j: (0, 0)),
        ],
        out_specs=(
            pl.BlockSpec((None, tm, d), lambda i, j: (i, jnp.minimum(j, last), 0)),
            pl.BlockSpec((None, tm, U_DIM), lambda i, j: (i, jnp.maximum(j - 1, 0), 0)),
            pl.BlockSpec((None, 2 * SSD_HEADS, tm), lambda i, j: (i, 0, jnp.maximum(j - 1, 0))),
        ),
        scratch_shapes=[pltpu.VMEM((tm, DT_PAD), F32), pltpu.VMEM((tm, d), F32)],
        compiler_params=_params("arbitrary", "arbitrary"),
        name="layer_boundary",
    )(u, u, u, y_na, u, y_ss, x, gate, g_post, conv_w, w_out, shift, scale, g_pre, w_u, w_dt)


def kernel(x, c, ctx, c_ctx, w_ada, b_ada, g_pre, g_post, w_in, conv_a_w, rpb, ssd_conv_w, ssd_conv_b,
           dt_bias_f, dt_bias_b, a_log_f, a_log_b, ssd_d, ssd_norm_w, w_out):
    depth = w_ada.shape[0]
    bsz, seq, d = x.shape
    n_ctx = ctx.shape[1]
    assert d == D_MODEL and seq % (2 * ATTN_ROWS * GRID_W) == 0 and seq // GRID_W >= ATTN_KEY_ROWS
    assert n_ctx % SSD_CHUNK == 0 and w_in.shape[-1] == U_DIM + 2 * SSD_HEADS
    tm_lat = min(512, seq)
    tm_edge = min(1024, seq)
    tm_ctx = n_ctx

    n_rows = -(-(bsz + 1) // 8) * 8
    cc = jnp.zeros((n_rows, d), F32).at[:bsz].set(c).at[bsz].set(c_ctx)
    mods = _ada_all_layers(cc, w_ada, b_ada)
    cos_t, sin_t = _rope_tables(seq)
    pad8 = DT_PAD - 2 * SSD_HEADS

    def layer_inputs(layer):
        m = mods[layer]
        lat = tuple(m[:bsz, k * d:(k + 1) * d].reshape(bsz, 1, d) for k in range(3))
        ctx_mod = tuple(jnp.broadcast_to(m[bsz, k * d:(k + 1) * d], (bsz, 1, d)) for k in range(3))
        w_u = w_in[layer, :, :U_DIM].astype(BF16)
        w_dt = jnp.pad(w_in[layer, :, U_DIM:], ((0, 0), (0, pad8))).astype(BF16)
        return lat, ctx_mod, w_u, w_dt, g_pre[layer].reshape(1, d)

    x_ctx = ctx
    (shift, scale, gate), (shift_c, scale_c, gate_c), w_u, w_dt, gpre = layer_inputs(0)
    u_lat, dt_lat = _inproj(x, shift, scale, gpre, w_u, w_dt, tm_edge)
    u_ctx, dt_ctx = _inproj(x_ctx, shift_c, scale_c, gpre, w_u, w_dt, tm_ctx)
    for layer in range(depth):
        gpost = g_post[layer].reshape(1, d)
        per_head = lambda f, bwd: jnp.broadcast_to(jnp.concatenate([f, bwd])[:, None], (2 * SSD_HEADS, SSD_CHUNK))
        dt_bias = per_head(dt_bias_f[layer], dt_bias_b[layer])
        a_log = per_head(a_log_f[layer], a_log_b[layer])
        d_vec = jnp.repeat(ssd_d[layer], SSD_HEAD_DIM).reshape(1, SSD_DIM)
        bias = _attn_bias_table(rpb[layer])

        y_na = _attention(u_lat, u_ctx, bias)
        y_ss, y_ss_ctx = _ssd(u_lat, dt_lat, u_ctx, dt_ctx, ssd_conv_w[layer], ssd_conv_b[layer].reshape(1, -1),
                              dt_bias, a_log, d_vec, ssd_norm_w[layer].reshape(1, -1), cos_t, sin_t)
        w_o = w_out[layer].astype(BF16)
        if layer == depth - 1:
            return _outproj(u_lat, y_na, y_ss, x, gate, gpost, conv_a_w[layer], w_o, tm_edge)
        (shift, scale, gate_next), (shift_c, scale_c, gate_c_next), w_u, w_dt, gpre = layer_inputs(layer + 1)
        x, u_lat, dt_lat = _layer_boundary(u_lat, y_na, y_ss, x, gate, gpost, conv_a_w[layer], w_o,
                                           shift, scale, gpre, w_u, w_dt, tm_lat)
        x_ctx, u_ctx, dt_ctx = _ctx_boundary(u_ctx, y_ss_ctx, x_ctx, gate_c, gpost, conv_a_w[layer], w_o,
                                             shift_c, scale_c, gpre, w_u, w_dt)
        gate, gate_c = gate_next, gate_c_next
```
